```python
import jax, jax.numpy as jnp
from jax import lax
import numpy as np

D_MODEL = 1024
BATCH = 16
SEQ = 2048
DEPTH = 2

EPS = 1e-6
SB_HEADS = 8
SB_HEAD_DIM = D_MODEL // 16
SB_WIDTH = SB_HEADS * SB_HEAD_DIM
SB_BLOCK = 128
GLA_HEADS = 4
GLA_DV = D_MODEL // 8
GLA_DK = GLA_DV // 2
GLA_KW = GLA_HEADS * GLA_DK
GLA_VW = GLA_HEADS * GLA_DV
GLA_GATE_RANK = 16
GLA_GATE_NORM = 16.0
GLA_CHUNK = 64
AB_SPLITS = [SB_WIDTH, 2 * SB_WIDTH, 3 * SB_WIDTH,
             3 * SB_WIDTH + GLA_KW, 3 * SB_WIDTH + 2 * GLA_KW,
             3 * SB_WIDTH + 2 * GLA_KW + GLA_VW,
             3 * SB_WIDTH + 2 * GLA_KW + 2 * GLA_VW]
AB_IN = 3 * SB_WIDTH + 2 * GLA_KW + 2 * GLA_VW + GLA_GATE_RANK
AB_MIX = SB_WIDTH + GLA_VW
SGU_WIDTH = 2 * D_MODEL
SGU_GROUPS = 8
SGU_GROUP_DIM = SGU_WIDTH // SGU_GROUPS
SGU_CHUNK = 128
PEER_HEADS = 8
PEER_TOPK = 16
N_KEYS = 128
N_EXPERTS = N_KEYS * N_KEYS
PEER_QDIM = 256
PEER_QHALF = PEER_QDIM // 2
PEER_TOKEN_BLOCK = 128

kernel_name = "hybrid_sb_gla_sgu_peer"


def rmsnorm(x, g):
    xf = x.astype(jnp.float32)
    y = xf * lax.rsqrt(jnp.mean(xf * xf, axis=-1, keepdims=True) + EPS)
    return (y * g.astype(jnp.float32)).astype(x.dtype)


def stick_breaking(q, k, v):
    b_, h_, s_, dh = q.shape
    nb = s_ // SB_BLOCK
    qb = q.astype(jnp.float32).reshape(b_, h_, nb, SB_BLOCK, dh).transpose(2, 0, 1, 3, 4)
    kf = k.astype(jnp.float32)
    vf = v.astype(jnp.float32)
    key_pos = jnp.arange(s_)
    scale = dh ** -0.5

    def one_block(args):
        qi, start = args
        z = jnp.einsum('bhqd,bhkd->bhqk', qi, kf) * scale
        q_pos = start + jnp.arange(SB_BLOCK)
        mask = key_pos[None, :] < q_pos[:, None]
        log_stay = jnp.where(mask, jax.nn.log_sigmoid(-z), 0.0)
        suffix = lax.cumsum(log_stay, axis=3, reverse=True) - log_stay
        a = jnp.where(mask, jnp.exp(jax.nn.log_sigmoid(z) + suffix), 0.0)
        return jnp.einsum('bhqk,bhkd->bhqd', a, vf)

    out = lax.map(one_block, (qb, jnp.arange(nb) * SB_BLOCK))
    return out.transpose(1, 0, 3, 2, 4).reshape(b_, s_, h_ * dh)


def gla(q, k, v, log_a):
    b_, h_, s_, dk = q.shape
    dv = v.shape[-1]
    n = s_ // GLA_CHUNK

    def ch(t):
        return t.reshape(b_, h_, n, GLA_CHUNK, t.shape[-1])

    q, k, v, log_a = ch(q), ch(k), ch(v), ch(log_a)
    bcum = jnp.cumsum(log_a, axis=3)
    q_in = q * (dk ** -0.5) * jnp.exp(bcum)
    k_in = k * jnp.exp(-bcum)
    causal = jnp.tril(jnp.ones((GLA_CHUNK, GLA_CHUNK), dtype=bool))
    scores = jnp.where(causal, jnp.einsum('bhnqd,bhnkd->bhnqk', q_in, k_in), 0.0)
    o_intra = jnp.einsum('bhnqk,bhnke->bhnqe', scores, v)
    b_last = bcum[:, :, :, -1:, :]
    kv = jnp.einsum('bhnkd,bhnke->bhnde', k * jnp.exp(b_last - bcum), v)
    decay = jnp.exp(b_last[:, :, :, 0, :])

    def step(state, inp):
        dec, kv_c = inp
        return dec[..., None] * state + kv_c, state

    init = jnp.zeros((b_, h_, dk, dv), jnp.float32)
    _, prev = lax.scan(step, init, (jnp.moveaxis(decay, 2, 0), jnp.moveaxis(kv, 2, 0)))
    prev = jnp.moveaxis(prev, 0, 2)
    o = o_intra + jnp.einsum('bhnqd,bhnde->bhnqe', q_in, prev)
    return o.reshape(b_, h_, s_, dv)


def layer_ab(h, w_in, w_alpha2, b_alpha2, gla_norm, w_out):
    b_, s_, _ = h.shape
    p = h @ w_in
    q_sb, k_sb, v_sb, q_g, k_g, v_g, g_g, a_lr = jnp.split(p, AB_SPLITS, axis=-1)

    def heads(t, nh):
        return t.reshape(b_, s_, nh, -1).transpose(0, 2, 1, 3)

    o_sb = stick_breaking(heads(q_sb, SB_HEADS), heads(k_sb, SB_HEADS), heads(v_sb, SB_HEADS))

    log_a = jax.nn.log_sigmoid((a_lr @ w_alpha2 + b_alpha2).astype(jnp.float32)) / GLA_GATE_NORM
    f32 = jnp.float32
    o_g = gla(heads(q_g, GLA_HEADS).astype(f32), heads(k_g, GLA_HEADS).astype(f32),
              heads(v_g, GLA_HEADS).astype(f32), heads(log_a, GLA_HEADS))
    o_g = o_g * lax.rsqrt(jnp.mean(o_g * o_g, axis=-1, keepdims=True) + EPS)
    o_g = o_g.transpose(0, 2, 1, 3).reshape(b_, s_, GLA_VW) * gla_norm.astype(f32)
    o_g = o_g * jax.nn.silu(g_g.astype(f32))

    mix = jnp.concatenate([o_sb, o_g], axis=-1).astype(h.dtype)
    return mix @ w_out


def layer_c(h, w_in, v_norm, w_spatial, b_spatial, w_out):
    b_, s_, _ = h.shape
    n = s_ // SGU_CHUNK
    z = jax.nn.gelu(h @ w_in, approximate=False)
    u, v = jnp.split(z, 2, axis=-1)
    v = rmsnorm(v, v_norm).reshape(b_, n, SGU_CHUNK, SGU_GROUPS, SGU_GROUP_DIM)
    causal = jnp.tril(jnp.ones((SGU_CHUNK, SGU_CHUNK), dtype=bool))
    w = jnp.where(causal[None], w_spatial, 0.0)
    mixed = jnp.einsum('gts,bnsgc->bntgc', w, v) + b_spatial.T[:, :, None]
    return (u * mixed.reshape(b_, s_, SGU_WIDTH)) @ w_out


def peer(h, w_q, subkeys, u_tab, v_tab):
    b_, s_, d = h.shape
    t_ = b_ * s_
    hf = h.reshape(t_, d)
    q = (hf @ w_q).astype(jnp.float32).reshape(t_, PEER_HEADS, 2, PEER_QHALF)
    scores = jnp.einsum('thpd,hpnd->thpn', q, subkeys.astype(jnp.float32))
    top_s, top_i = lax.top_k(scores, PEER_TOPK)
    cand_s = (top_s[:, :, 0, :, None] + top_s[:, :, 1, None, :]).reshape(t_, PEER_HEADS, -1)
    cand_i = (top_i[:, :, 0, :, None] * N_KEYS + top_i[:, :, 1, None, :]).reshape(t_, PEER_HEADS, -1)
    best_s, pos = lax.top_k(cand_s, PEER_TOPK)
    idx = jnp.take_along_axis(cand_i, pos, axis=-1)
    gate = jax.nn.softmax(best_s, axis=-1)

    nblk = t_ // PEER_TOKEN_BLOCK
    xb = hf.reshape(nblk, PEER_TOKEN_BLOCK, d)
    ib = idx.reshape(nblk, PEER_TOKEN_BLOCK, PEER_HEADS * PEER_TOPK)
    gb = gate.reshape(nblk, PEER_TOKEN_BLOCK, PEER_HEADS * PEER_TOPK)

    def expert_block(args):
        xi, ii, gi = args
        u_sel = u_tab[ii]
        act = jax.nn.gelu(jnp.einsum('td,tkd->tk', xi, u_sel).astype(jnp.float32), approximate=False)
        w = (gi * act).astype(v_tab.dtype)
        return jnp.einsum('tk,tkd->td', w, v_tab[ii])

    out = lax.map(expert_block, (xb, ib, gb))
    return out.reshape(b_, s_, d).astype(h.dtype)


def setup_inputs(seed: int = 0) -> dict:
    key = jax.random.key(seed)
    ks = jax.random.split(key, 20)
    n_even = (DEPTH + 1) // 2
    n_odd = DEPTH // 2
    nrm = jax.random.normal
    f32 = jnp.float32
    return {
        "x": nrm(ks[0], (BATCH, SEQ, D_MODEL), f32),
        "norm_mix": 1.0 + 0.02 * nrm(ks[1], (DEPTH, D_MODEL), f32),
        "norm_ffn": 1.0 + 0.02 * nrm(ks[2], (DEPTH, D_MODEL), f32),
        "ab_w_in": nrm(ks[3], (n_even, D_MODEL, AB_IN), f32) * D_MODEL ** -0.5,
        "ab_w_alpha2": nrm(ks[4], (n_even, GLA_GATE_RANK, GLA_KW), f32) * GLA_GATE_RANK ** -0.5,
        "ab_b_alpha2": 0.1 * nrm(ks[5], (n_even, GLA_KW), f32),
        "ab_gla_norm": 1.0 + 0.02 * nrm(ks[6], (n_even, GLA_VW), f32),
        "ab_w_out": nrm(ks[7], (n_even, AB_MIX, D_MODEL), f32) * AB_MIX ** -0.5,
        "c_w_in": nrm(ks[8], (n_odd, D_MODEL, 2 * SGU_WIDTH), f32) * D_MODEL ** -0.5,
        "c_v_norm": 1.0 + 0.02 * nrm(ks[9], (n_odd, SGU_WIDTH), f32),
        "c_w_spatial": nrm(ks[10], (n_odd, SGU_GROUPS, SGU_CHUNK, SGU_CHUNK), f32) * SGU_CHUNK ** -0.5,
        "c_b_spatial": 1.0 + 0.02 * nrm(ks[11], (n_odd, SGU_GROUPS, SGU_CHUNK), f32),
        "c_w_out": nrm(ks[12], (n_odd, SGU_WIDTH, D_MODEL), f32) * SGU_WIDTH ** -0.5,
        "peer_w_q": nrm(ks[13], (DEPTH, D_MODEL, PEER_HEADS * PEER_QDIM), f32) * D_MODEL ** -0.5,
        "peer_subkeys": nrm(ks[14], (DEPTH, PEER_HEADS, 2, N_KEYS, PEER_QHALF), f32) * PEER_QHALF ** -0.5,
        "peer_u": nrm(ks[15], (DEPTH, N_EXPERTS, D_MODEL), f32) * D_MODEL ** -0.5,
        "peer_v": nrm(ks[16], (DEPTH, N_EXPERTS, D_MODEL), f32) * PEER_HEADS ** -0.5,
        "final_norm": 1.0 + 0.02 * nrm(ks[17], (D_MODEL,), f32),
    }


def reference(x, norm_mix, norm_ffn, ab_w_in, ab_w_alpha2, ab_b_alpha2, ab_gla_norm, ab_w_out,
              c_w_in, c_v_norm, c_w_spatial, c_b_spatial, c_w_out,
              peer_w_q, peer_subkeys, peer_u, peer_v, final_norm):
    for i in range(DEPTH):
        j = i // 2
        h = rmsnorm(x, norm_mix[i])
        if i % 2 == 0:
            x = x + layer_ab(h, ab_w_in[j], ab_w_alpha2[j], ab_b_alpha2[j], ab_gla_norm[j], ab_w_out[j])
        else:
            x = x + layer_c(h, c_w_in[j], c_v_norm[j], c_w_spatial[j], c_b_spatial[j], c_w_out[j])
        x = x + peer(rmsnorm(x, norm_ffn[i]), peer_w_q[i], peer_subkeys[i], peer_u[i], peer_v[i])
    return rmsnorm(x, final_norm)
```

```python
import functools

import jax
import jax.numpy as jnp
from jax import lax
from jax.experimental import pallas as pl
from jax.experimental.pallas import tpu as pltpu

F32 = jnp.float32
BF16 = jnp.bfloat16
EPS = 1e-6

D_MODEL = 1024
SB_WIDTH = 512
SB_BLOCK = 128
GLA_KW = 256
GLA_VW = 512
GLA_CHUNK = 64
GLA_GATE_NORM = 16.0
SGU_WIDTH = 2048
SGU_GROUPS = 8
SGU_GROUP_DIM = 256
SGU_CHUNK = 128
PEER_HEADS = 8
PEER_TOPK = 16
N_KEYS = 128
N_EXPERTS = N_KEYS * N_KEYS
PEER_HK = PEER_HEADS * PEER_TOPK
LANES = 128
SUBLANES = 8
ROWS_PER_EXPERT = D_MODEL // (2 * LANES)
VMEM_LIMIT = 56 * 1024 * 1024

_HIGH = lax.Precision.HIGHEST


def _params(*sem, vmem=VMEM_LIMIT):
    return pltpu.CompilerParams(dimension_semantics=sem, vmem_limit_bytes=vmem)


def _rms(x, g):
    return x * lax.rsqrt(jnp.mean(x * x, axis=-1, keepdims=True) + EPS) * g


def _softplus(z):
    return jnp.maximum(z, 0.0) + jnp.log(1.0 + jnp.exp(-jnp.abs(z)))


def _gelu(x):
    return 0.5 * x * (1.0 + lax.erf(x * 0.7071067811865476))


def _dot_nt(a, b, **kw):
    return lax.dot_general(a, b, (((1,), (1,)), ((), ())), preferred_element_type=F32, **kw)


def _dot(a, b, **kw):
    return jnp.dot(a, b, preferred_element_type=F32, **kw)


def _ab_in_kernel(x_ref, g_ref, w_ref, wa_ref, w2_ref, b2_ref,
                  qs_ref, ks_ref, vs_ref, qg_ref, kg_ref, vg_ref, gg_ref, la_ref):
    h = _rms(x_ref[...], g_ref[...]).astype(BF16)
    qs_ref[...] = (_dot(h, w_ref[:, 0:512]) * 0.125).astype(BF16)
    ks_ref[...] = _dot(h, w_ref[:, 512:1024]).astype(BF16)
    vs_ref[...] = _dot(h, w_ref[:, 1024:1536]).astype(BF16)
    qg_ref[...] = _dot(h, w_ref[:, 1536:1792])
    kg_ref[...] = _dot(h, w_ref[:, 1792:2048])
    vg_ref[...] = _dot(h, w_ref[:, 2048:2560]).astype(BF16)
    gg_ref[...] = _dot(h, w_ref[:, 2560:3072])
    a_lr = _dot(h, wa_ref[...])
    pre = _dot(a_lr.astype(BF16), w2_ref[...]) + b2_ref[...]
    la_ref[...] = -_softplus(-pre) * (1.0 / GLA_GATE_NORM)


def _ab_in(x2, g, w_in, w_alpha2, b_alpha2, tm=512):
    t = x2.shape[0]
    w_main = w_in[:, :3072].astype(BF16)
    wa = jnp.pad(w_in[:, 3072:], ((0, 0), (0, LANES - 16))).astype(BF16)
    w2 = jnp.pad(w_alpha2, ((0, LANES - 16), (0, 0))).astype(BF16)
    row = lambda i: (i, 0)
    fix = lambda i: (0, 0)
    outs = [(512, BF16), (512, BF16), (512, BF16), (256, F32), (256, F32), (512, BF16), (512, F32), (256, F32)]
    return pl.pallas_call(
        _ab_in_kernel,
        grid=(t // tm,),
        in_specs=[pl.BlockSpec((tm, D_MODEL), row), pl.BlockSpec((1, D_MODEL), fix),
                  pl.BlockSpec((D_MODEL, 3072), fix), pl.BlockSpec((D_MODEL, LANES), fix),
                  pl.BlockSpec((LANES, GLA_KW), fix), pl.BlockSpec((1, GLA_KW), fix)],
        out_specs=[pl.BlockSpec((tm, n), row) for n, _ in outs],
        out_shape=[jax.ShapeDtypeStruct((t, n), d) for n, d in outs],
        compiler_params=_params("arbitrary"),
        name="ab_in",
    )(x2, g.reshape(1, -1), w_main, wa, w2, b_alpha2.reshape(1, -1))


def _sb_kernel(q_ref, k_ref, v_ref, o_ref):
    i = pl.program_id(2)
    blk = SB_BLOCK
    lane = lax.broadcasted_iota(jnp.int32, (blk, blk), 1)
    row = lax.broadcasted_iota(jnp.int32, (blk, blk), 0)
    head0 = lane < 64
    q = q_ref[...]
    zero = jnp.zeros_like(q)
    qh = (jnp.where(head0, q, zero), jnp.where(head0, zero, q))
    r2 = lax.broadcasted_iota(jnp.int32, (blk, 2 * blk), 0)
    c2 = lax.broadcasted_iota(jnp.int32, (blk, 2 * blk), 1)
    uo = jnp.where((r2 > c2) | (c2 >= blk), 1.0, 0.0).astype(BF16)

    def block(j, carry, diag):
        c0, c1, acc = carry
        start = pl.multiple_of(j * blk, blk)
        kj = k_ref[pl.ds(start, blk), :]
        vj = v_ref[pl.ds(start, blk), :]
        pvs = []
        cs = []
        for h, c in ((0, c0), (1, c1)):
            z = _dot_nt(qh[h], kj)
            sp = _softplus(z)
            ls = -sp
            if diag:
                ls = jnp.where(lane < row, ls, 0.0)
            ls_hi = ls.astype(BF16)
            ls_lo = (ls - ls_hi.astype(F32)).astype(BF16)
            s2 = _dot(ls_hi, uo) + _dot(ls_lo, uo)
            a = jnp.exp((z - sp) + s2[:, :blk] + c)
            if diag:
                a = jnp.where(lane < row, a, 0.0)
            pvs.append(_dot(a.astype(BF16), vj))
            cs.append(c + s2[:, blk:])
        acc = acc + jnp.where(head0, pvs[0], pvs[1])
        return cs[0], cs[1], acc

    zf = jnp.zeros((blk, blk), F32)
    carry = block(i, (zf, zf, zf), True)
    carry = lax.fori_loop(0, i, lambda it, c: block(i - 1 - it, c, False), carry)
    o_ref[...] = carry[2].astype(o_ref.dtype)


def _sb_attn(q, k, v, batch, seq):
    t = q.shape[0]
    nq = seq // SB_BLOCK
    return pl.pallas_call(
        _sb_kernel,
        grid=(batch, SB_WIDTH // LANES, nq),
        in_specs=[pl.BlockSpec((SB_BLOCK, LANES), lambda b, p, i: (b * nq + i, p)),
                  pl.BlockSpec((seq, LANES), lambda b, p, i: (b, p)),
                  pl.BlockSpec((seq, LANES), lambda b, p, i: (b, p))],
        out_specs=pl.BlockSpec((SB_BLOCK, LANES), lambda b, p, i: (b * nq + i, p)),
        out_shape=jax.ShapeDtypeStruct((t, SB_WIDTH), BF16),
        compiler_params=_params("arbitrary", "arbitrary", "arbitrary"),
        name="sb_attn",
    )(q, k, v)


def _gla_kernel(q_ref, k_ref, la_ref, v_ref, g_ref, gn_ref, o_ref):
    ch = GLA_CHUNK
    seq = q_ref.shape[0]
    ri = lax.broadcasted_iota(jnp.int32, (ch, ch), 0)
    ci = lax.broadcasted_iota(jnp.int32, (ch, ch), 1)
    causal = ci <= ri
    tri = jnp.where(causal, 1.0, 0.0).astype(F32)
    lane = lax.broadcasted_iota(jnp.int32, (ch, LANES), 1)
    head0 = lane < 64
    sr = lax.broadcasted_iota(jnp.int32, (2 * LANES, LANES), 0)
    sc = lax.broadcasted_iota(jnp.int32, (2 * LANES, LANES), 1)
    same_head = (sr >= LANES) == (sc >= 64)
    gn = gn_ref[...]

    def chunk(c, state_t):
        r0 = pl.multiple_of(c * ch, ch)
        la = la_ref[pl.ds(r0, ch), :]
        bcum = _dot(tri, la, precision=_HIGH)
        b_last = bcum[ch - 1:ch, :]
        q_in = q_ref[pl.ds(r0, ch), :] * 0.125 * jnp.exp(bcum)
        kc = k_ref[pl.ds(r0, ch), :]
        k_in = (kc * jnp.exp(-bcum)).astype(BF16)
        k_dec = (kc * jnp.exp(b_last - bcum)).astype(BF16)
        vc = v_ref[pl.ds(r0, ch), :]
        qb = q_in.astype(BF16)
        zero = jnp.zeros_like(qb)
        outs = []
        for h in range(2):
            qm = jnp.where(head0, qb, zero) if h == 0 else jnp.where(head0, zero, qb)
            sc_h = jnp.where(causal, _dot_nt(qm, k_in), 0.0)
            outs.append(_dot(sc_h.astype(BF16), vc[:, h * LANES:(h + 1) * LANES]))
        o = jnp.concatenate(outs, axis=1) + _dot_nt(qb, state_t.astype(BF16))
        kv_t = lax.dot_general(vc, k_dec, (((0,), (0,)), ((), ())), preferred_element_type=F32)
        state_t = state_t * jnp.exp(b_last) + jnp.where(same_head, kv_t, 0.0)
        halves = []
        for h in range(2):
            oh = o[:, h * LANES:(h + 1) * LANES]
            halves.append(oh * lax.rsqrt(jnp.mean(oh * oh, axis=-1, keepdims=True) + EPS))
        on = jnp.concatenate(halves, axis=1) * gn
        gg = g_ref[pl.ds(r0, ch), :]
        o_ref[pl.ds(r0, ch), :] = (on * (gg * jax.nn.sigmoid(gg))).astype(o_ref.dtype)
        return state_t

    lax.fori_loop(0, seq // ch, chunk, jnp.zeros((2 * LANES, LANES), F32))


def _gla(qg, kg, la, vg, gg, gla_norm, batch, seq):
    t = qg.shape[0]
    bp = lambda b, p: (b, p)
    return pl.pallas_call(
        _gla_kernel,
        grid=(batch, 2),
        in_specs=[pl.BlockSpec((seq, LANES), bp), pl.BlockSpec((seq, LANES), bp),
                  pl.BlockSpec((seq, LANES), bp), pl.BlockSpec((seq, 2 * LANES), bp),
                  pl.BlockSpec((seq, 2 * LANES), bp), pl.BlockSpec((1, 2 * LANES), lambda b, p: (0, p))],
        out_specs=pl.BlockSpec((seq, 2 * LANES), bp),
        out_shape=jax.ShapeDtypeStruct((t, GLA_VW), BF16),
        compiler_params=_params("arbitrary", "arbitrary"),
        name="gla",
    )(qg, kg, la, vg, gg, gla_norm.reshape(1, -1))


def _ab_out_kernel(x_ref, a_ref, b_ref, wa_ref, wb_ref, o_ref):
    o_ref[...] = x_ref[...] + _dot(a_ref[...], wa_ref[...]) + _dot(b_ref[...], wb_ref[...])


def _ab_out(x2, o_sb, o_g, w_out, tm=512):
    t = x2.shape[0]
    w = w_out.astype(BF16)
    row = lambda i: (i, 0)
    fix = lambda i: (0, 0)
    return pl.pallas_call(
        _ab_out_kernel,
        grid=(t // tm,),
        in_specs=[pl.BlockSpec((tm, D_MODEL), row), pl.BlockSpec((tm, SB_WIDTH), row),
                  pl.BlockSpec((tm, GLA_VW), row), pl.BlockSpec((SB_WIDTH, D_MODEL), fix),
                  pl.BlockSpec((GLA_VW, D_MODEL), fix)],
        out_specs=pl.BlockSpec((tm, D_MODEL), row),
        out_shape=jax.ShapeDtypeStruct((t, D_MODEL), F32),
        compiler_params=_params("arbitrary"),
        name="ab_out",
    )(x2, o_sb, o_g, w[:SB_WIDTH], w[SB_WIDTH:])


def _sgu_kernel(x_ref, g_ref, win_ref, vn_ref, ws_ref, bs_ref, wout_ref, o_ref, y_ref):
    tm = x_ref.shape[0]
    x = x_ref[...]
    h = _rms(x, g_ref[...]).astype(BF16)
    ri = lax.broadcasted_iota(jnp.int32, (SGU_CHUNK, SGU_CHUNK), 0)
    ci = lax.broadcasted_iota(jnp.int32, (SGU_CHUNK, SGU_CHUNK), 1)
    causal = ci <= ri
    v = _gelu(_dot(h, win_ref[:, SGU_WIDTH:]))
    v = (_rms(v, vn_ref[...])).astype(BF16)
    for g in range(SGU_GROUPS):
        cols = slice(g * SGU_GROUP_DIM, (g + 1) * SGU_GROUP_DIM)
        wg = jnp.where(causal, ws_ref[g], 0.0).astype(BF16)
        u_g = _gelu(_dot(h, win_ref[:, cols]))
        bias = bs_ref[g]
        for c in range(tm // SGU_CHUNK):
            rows = slice(c * SGU_CHUNK, (c + 1) * SGU_CHUNK)
            mixed = _dot(wg, v[rows, cols]) + jnp.concatenate([bias, bias], axis=1)
            y_ref[rows, cols] = (u_g[rows] * mixed).astype(BF16)
    o_ref[...] = x + _dot(y_ref[...], wout_ref[...])


def _sgu(x2, g, w_in, v_norm, w_spatial, b_spatial, w_out, tm=256):
    t = x2.shape[0]
    row = lambda i: (i, 0)
    fix = lambda i: (0, 0)
    fix3 = lambda i: (0, 0, 0)
    bias = jnp.broadcast_to(b_spatial[:, :, None], (SGU_GROUPS, SGU_CHUNK, LANES))
    return pl.pallas_call(
        _sgu_kernel,
        grid=(t // tm,),
        in_specs=[pl.BlockSpec((tm, D_MODEL), row), pl.BlockSpec((1, D_MODEL), fix),
                  pl.BlockSpec((D_MODEL, 2 * SGU_WIDTH), fix), pl.BlockSpec((1, SGU_WIDTH), fix),
                  pl.BlockSpec((SGU_GROUPS, SGU_CHUNK, SGU_CHUNK), fix3),
                  pl.BlockSpec((SGU_GROUPS, SGU_CHUNK, LANES), fix3),
                  pl.BlockSpec((SGU_WIDTH, D_MODEL), fix)],
        out_specs=pl.BlockSpec((tm, D_MODEL), row),
        out_shape=jax.ShapeDtypeStruct((t, D_MODEL), F32),
        scratch_shapes=[pltpu.VMEM((tm, SGU_WIDTH), BF16)],
        compiler_params=_params("arbitrary"),
        name="sgu",
    )(x2, g.reshape(1, -1), w_in.astype(BF16), v_norm.reshape(1, -1), w_spatial, bias, w_out.astype(BF16))


def _top16(s, n_rows):
    tm = s.shape[1]
    rid = lax.broadcasted_iota(jnp.int32, (n_rows, tm), 0)
    vals, idxs = [], []
    for _ in range(PEER_TOPK):
        m = jnp.max(s, axis=0, keepdims=True)
        first = jnp.min(jnp.where(s == m, rid, n_rows), axis=0, keepdims=True)
        vals.append(m)
        idxs.append(first)
        s = jnp.where(rid == first, -jnp.inf, s)
    return jnp.concatenate(vals, axis=0), jnp.concatenate(idxs, axis=0)


def _route_kernel(x_ref, g_ref, wq_ref, sk_ref, h_ref, idx_ref, gate_ref):
    tm = x_ref.shape[0]
    h = _rms(x_ref[...], g_ref[...])
    h_ref[...] = h
    q = _dot(h.astype(BF16), wq_ref[...]).astype(BF16)
    k16 = lax.broadcasted_iota(jnp.int32, (PEER_TOPK, tm), 0)
    idx_rows, gate_rows = [], []
    for hd in range(PEER_HEADS):
        top = []
        for p in range(2):
            j = hd * 2 + p
            s = _dot_nt(sk_ref[j], q[:, j * N_KEYS:(j + 1) * N_KEYS])
            top.append(_top16(s, N_KEYS))
        (s0, i0), (s1, i1) = top
        cand = jnp.concatenate([s0[a:a + 1, :] + s1 for a in range(PEER_TOPK)], axis=0)
        best, pos = _top16(cand, PEER_TOPK * PEER_TOPK)
        rows = []
        for r in range(PEER_TOPK):
            pa = pos[r:r + 1, :] >> 4
            pb = pos[r:r + 1, :] & 15
            ea = jnp.sum(jnp.where(k16 == pa, i0, 0), axis=0, keepdims=True)
            eb = jnp.sum(jnp.where(k16 == pb, i1, 0), axis=0, keepdims=True)
            rows.append(ea * N_KEYS + eb)
        idx_rows.append(jnp.concatenate(rows, axis=0))
        e = jnp.exp(best - best[0:1, :])
        gate_rows.append(e / jnp.sum(e, axis=0, keepdims=True))
    idx_t = jnp.concatenate(idx_rows, axis=0)
    gate_t = jnp.concatenate(gate_rows, axis=0)
    idx_ref[...] = (idx_t * ROWS_PER_EXPERT).T
    gate_ref[...] = gate_t.T


def _route(x2, g, w_q, subkeys, tm=128):
    t = x2.shape[0]
    row = lambda i: (i, 0)
    fix = lambda i: (0, 0)
    sk = subkeys.reshape(PEER_HEADS * 2, N_KEYS, N_KEYS).astype(BF16)
    return pl.pallas_call(
        _route_kernel,
        grid=(t // tm,),
        in_specs=[pl.BlockSpec((tm, D_MODEL), row), pl.BlockSpec((1, D_MODEL), fix),
                  pl.BlockSpec((D_MODEL, 2 * PEER_HEADS * N_KEYS), fix),
                  pl.BlockSpec((PEER_HEADS * 2, N_KEYS, N_KEYS), lambda i: (0, 0, 0))],
        out_specs=[pl.BlockSpec((tm, D_MODEL), row), pl.BlockSpec((tm, PEER_HK), row),
                   pl.BlockSpec((tm, PEER_HK), row)],
        out_shape=[jax.ShapeDtypeStruct((t, D_MODEL), F32), jax.ShapeDtypeStruct((t, PEER_HK), jnp.int32),
                   jax.ShapeDtypeStruct((t, PEER_HK), F32)],
        compiler_params=_params("arbitrary"),
        name="peer_route",
    )(x2, g.reshape(1, -1), w_q.astype(BF16), sk)


def _pack_table(tab):
    n = tab.shape[0]
    b = lax.bitcast_convert_type(tab.astype(BF16), jnp.uint16).astype(jnp.uint32)
    b = b.reshape(n, ROWS_PER_EXPERT, 2, LANES)
    return (b[:, :, 0, :] | (b[:, :, 1, :] << 16)).reshape(n * ROWS_PER_EXPERT, LANES)


def _load_table(tab_hbm, tab_vmem, sem):
    @pl.when(pl.program_id(0) == 0)
    def _():
        cp = pltpu.make_async_copy(tab_hbm, tab_vmem, sem)
        cp.start()
        cp.wait()


def _expert_tile(tab_vmem, row):
    wd = tab_vmem[pl.ds(pl.multiple_of(row, ROWS_PER_EXPERT), ROWS_PER_EXPERT), :]
    return pltpu.bitcast(wd, BF16).astype(F32)


def _fold(x, y, half):
    sub = lax.broadcasted_iota(jnp.int32, (SUBLANES, LANES), 0)
    low = (sub & half) == 0
    if half == 4:
        return jnp.where(low, x, y) + pltpu.roll(jnp.where(low, y, x), 4, axis=0)
    xs = x + pltpu.roll(x, SUBLANES - half, axis=0)
    ys = y + pltpu.roll(y, half, axis=0)
    return jnp.where(low, xs, ys)


def _peer_u_kernel(idx_ref, x3_ref, gate_ref, tab_hbm, w_ref, tab_vmem, act_ref, sem):
    _load_table(tab_hbm, tab_vmem, sem)
    tb = x3_ref.shape[0]
    ones = jnp.ones((SUBLANES, LANES), F32)

    def token(t, carry):
        xt = x3_ref[t]
        folded = []
        for k0 in range(0, PEER_HK, SUBLANES):
            p = [_expert_tile(tab_vmem, idx_ref[t, k0 + j]) * xt for j in range(SUBLANES)]
            z = [_fold(p[j], p[j + 4], 4) for j in range(4)]
            z = [_fold(z[0], z[2], 2), _fold(z[1], z[3], 2)]
            folded.append(_fold(z[0], z[1], 1))
        r = jnp.concatenate(folded, axis=0)
        act_ref[pl.ds(t, 1), :] = _dot_nt(ones, r, precision=_HIGH)[0:1, :]
        return carry

    lax.fori_loop(0, tb, token, 0)
    act = act_ref[...]
    w_ref[...] = gate_ref[...] * _gelu(act)


def _peer_u(idx4, h3, gate, tab, tb=32):
    t = idx4.shape[0]
    row = lambda i: (i, 0)
    return pl.pallas_call(
        _peer_u_kernel,
        grid=(t // tb,),
        in_specs=[pl.BlockSpec((tb, PEER_HK), row, memory_space=pltpu.SMEM),
                  pl.BlockSpec((tb, SUBLANES, LANES), lambda i: (i, 0, 0)),
                  pl.BlockSpec((tb, PEER_HK), row),
                  pl.BlockSpec(memory_space=pl.ANY)],
        out_specs=pl.BlockSpec((tb, PEER_HK), row),
        out_shape=jax.ShapeDtypeStruct((t, PEER_HK), F32),
        scratch_shapes=[pltpu.VMEM((N_EXPERTS * ROWS_PER_EXPERT, LANES), jnp.uint32),
                        pltpu.VMEM((tb, PEER_HK), F32),
                        pltpu.SemaphoreType.DMA],
        compiler_params=_params("arbitrary"),
        name="peer_u",
    )(idx4, h3, gate, tab)


def _peer_v_kernel(idx_ref, w_ref, x3_ref, tab_hbm, o_ref, tab_vmem, sem):
    _load_table(tab_hbm, tab_vmem, sem)
    tb = x3_ref.shape[0]
    nacc = 4

    def token(t, carry):
        acc = [jnp.zeros((SUBLANES, LANES), F32) for _ in range(nacc)]
        for k in range(PEER_HK):
            acc[k % nacc] = acc[k % nacc] + w_ref[t, k] * _expert_tile(tab_vmem, idx_ref[t, k])
        o_ref[t] = x3_ref[t] + ((acc[0] + acc[1]) + (acc[2] + acc[3]))
        return carry

    lax.fori_loop(0, tb, token, 0)


def _peer_v(idx4, w, x3, tab, tb=32):
    t = idx4.shape[0]
    row = lambda i: (i, 0)
    tile = pl.BlockSpec((tb, SUBLANES, LANES), lambda i: (i, 0, 0))
    return pl.pallas_call(
        _peer_v_kernel,
        grid=(t // tb,),
        in_specs=[pl.BlockSpec((tb, PEER_HK), row, memory_space=pltpu.SMEM),
                  pl.BlockSpec((tb, PEER_HK), row, memory_space=pltpu.SMEM),
                  tile,
                  pl.BlockSpec(memory_space=pl.ANY)],
        out_specs=tile,
        out_shape=jax.ShapeDtypeStruct((t, SUBLANES, LANES), F32),
        scratch_shapes=[pltpu.VMEM((N_EXPERTS * ROWS_PER_EXPERT, LANES), jnp.uint32),
                        pltpu.SemaphoreType.DMA],
        compiler_params=_params("arbitrary"),
        name="peer_v",
    )(idx4, w, x3, tab)


def _peer(x2, g, w_q, subkeys, u_tab, v_tab):
    t = x2.shape[0]
    h, idx4, gate = _route(x2, g, w_q, subkeys)
    w = _peer_u(idx4, h.reshape(t, SUBLANES, LANES), gate, _pack_table(u_tab))
    out3 = _peer_v(idx4, w, x2.reshape(t, SUBLANES, LANES), _pack_table(v_tab))
    return out3.reshape(t, D_MODEL)


def _final_kernel(x_ref, g_ref, o_ref):
    o_ref[...] = _rms(x_ref[...], g_ref[...])


def _final_norm(x2, g, tm=1024):
    t = x2.shape[0]
    return pl.pallas_call(
        _final_kernel,
        grid=(t // tm,),
        in_specs=[pl.BlockSpec((tm, D_MODEL), lambda i: (i, 0)), pl.BlockSpec((1, D_MODEL), lambda i: (0, 0))],
        out_specs=pl.BlockSpec((tm, D_MODEL), lambda i: (i, 0)),
        out_shape=jax.ShapeDtypeStruct((t, D_MODEL), F32),
        compiler_params=_params("arbitrary"),
        name="final_norm",
    )(x2, g.reshape(1, -1))


def kernel(x, norm_mix, norm_ffn, ab_w_in, ab_w_alpha2, ab_b_alpha2, ab_gla_norm, ab_w_out, c_w_in, c_v_norm, c_w_spatial, c_b_spatial, c_w_out, peer_w_q, peer_subkeys, peer_u, peer_v, final_norm):
    batch, seq, d = x.shape
    x2 = x.reshape(batch * seq, d)
    depth = norm_mix.shape[0]
    for i in range(depth):
        j = i // 2
        if i % 2 == 0:
            qs, ks, vs, qg, kg, vg, gg, la = _ab_in(x2, norm_mix[i], ab_w_in[j], ab_w_alpha2[j], ab_b_alpha2[j])
            o_sb = _sb_attn(qs, ks, vs, batch, seq)
            o_g = _gla(qg, kg, la, vg, gg, ab_gla_norm[j], batch, seq)
            x2 = _ab_out(x2, o_sb, o_g, ab_w_out[j])
        else:
            x2 = _sgu(x2, norm_mix[i], c_w_in[j], c_v_norm[j], c_w_spatial[j], c_b_spatial[j], c_w_out[j])
        x2 = _peer(x2, norm_ffn[i], peer_w_q[i], peer_subkeys[i], peer_u[i], peer_v[i])
    return _final_norm(x2, final_norm).reshape(batch, seq, d)
```

```python
import functools

import jax
import jax.numpy as jnp
from jax import lax
from jax.experimental import pallas as pl
from jax.experimental.pallas import tpu as pltpu

F32 = jnp.float32
BF16 = jnp.bfloat16
EPS = 1e-6

D_MODEL = 1024
SB_WIDTH = 512
SB_BLOCK = 128
GLA_KW = 256
GLA_VW = 512
GLA_CHUNK = 64
GLA_GATE_NORM = 16.0
SGU_WIDTH = 2048
SGU_GROUPS = 8
SGU_GROUP_DIM = 256
SGU_CHUNK = 128
PEER_HEADS = 8
PEER_TOPK = 16
N_KEYS = 128
N_EXPERTS = N_KEYS * N_KEYS
PEER_HK = PEER_HEADS * PEER_TOPK
PEER_TB = 128
LANES = 128
SUBLANES = 8
ROWS_PER_EXPERT = D_MODEL // (2 * LANES)
VMEM_LIMIT = 56 * 1024 * 1024

_HIGH = lax.Precision.HIGHEST


def _params(*sem, vmem=VMEM_LIMIT):
    return pltpu.CompilerParams(dimension_semantics=sem, vmem_limit_bytes=vmem)


def _rms(x, g):
    return x * lax.rsqrt(jnp.mean(x * x, axis=-1, keepdims=True) + EPS) * g


def _softplus(z):
    return jnp.maximum(z, 0.0) + jnp.log(1.0 + jnp.exp(-jnp.abs(z)))


def _gelu(x):
    return 0.5 * x * (1.0 + lax.erf(x * 0.7071067811865476))


def _dot_nt(a, b, **kw):
    return lax.dot_general(a, b, (((1,), (1,)), ((), ())), preferred_element_type=F32, **kw)


def _dot(a, b, **kw):
    return jnp.dot(a, b, preferred_element_type=F32, **kw)


def _ab_in_kernel(x_ref, g_ref, w_ref, wa_ref, w2_ref, b2_ref,
                  qs_ref, ks_ref, vs_ref, qg_ref, kg_ref, vg_ref, gg_ref, la_ref):
    h = _rms(x_ref[...], g_ref[...]).astype(BF16)
    qs_ref[...] = (_dot(h, w_ref[:, 0:512]) * 0.125).astype(BF16)
    ks_ref[...] = _dot(h, w_ref[:, 512:1024]).astype(BF16)
    vs_ref[...] = _dot(h, w_ref[:, 1024:1536]).astype(BF16)
    qg_ref[...] = _dot(h, w_ref[:, 1536:1792])
    kg_ref[...] = _dot(h, w_ref[:, 1792:2048])
    vg_ref[...] = _dot(h, w_ref[:, 2048:2560]).astype(BF16)
    gg_ref[...] = _dot(h, w_ref[:, 2560:3072])
    a_lr = _dot(h, wa_ref[...])
    pre = _dot(a_lr.astype(BF16), w2_ref[...]) + b2_ref[...]
    la_ref[...] = -_softplus(-pre) * (1.0 / GLA_GATE_NORM)


def _ab_in(x2, g, w_in, w_alpha2, b_alpha2, tm=512):
    t = x2.shape[0]
    w_main = w_in[:, :3072].astype(BF16)
    wa = jnp.pad(w_in[:, 3072:], ((0, 0), (0, LANES - 16))).astype(BF16)
    w2 = jnp.pad(w_alpha2, ((0, LANES - 16), (0, 0))).astype(BF16)
    row = lambda i: (i, 0)
    fix = lambda i: (0, 0)
    outs = [(512, BF16), (512, BF16), (512, BF16), (256, F32), (256, F32), (512, BF16), (512, F32), (256, F32)]
    return pl.pallas_call(
        _ab_in_kernel,
        grid=(t // tm,),
        in_specs=[pl.BlockSpec((tm, D_MODEL), row), pl.BlockSpec((1, D_MODEL), fix),
                  pl.BlockSpec((D_MODEL, 3072), fix), pl.BlockSpec((D_MODEL, LANES), fix),
                  pl.BlockSpec((LANES, GLA_KW), fix), pl.BlockSpec((1, GLA_KW), fix)],
        out_specs=[pl.BlockSpec((tm, n), row) for n, _ in outs],
        out_shape=[jax.ShapeDtypeStruct((t, n), d) for n, d in outs],
        compiler_params=_params("arbitrary"),
        name="ab_in",
    )(x2, g.reshape(1, -1), w_main, wa, w2, b_alpha2.reshape(1, -1))


def _sb_kernel(q_ref, k_ref, v_ref, o_ref, c_ref, acc_ref):
    qi = pl.program_id(2)
    blk = SB_BLOCK
    qt = q_ref.shape[0]
    kb_per_tile = qt // blk
    lane = lax.broadcasted_iota(jnp.int32, (qt, blk), 1)
    row = lax.broadcasted_iota(jnp.int32, (qt, blk), 0)
    head0 = lane < 64
    q = q_ref[...]
    zero = jnp.zeros_like(q)
    qh = (jnp.where(head0, q, zero), jnp.where(head0, zero, q))
    r2 = lax.broadcasted_iota(jnp.int32, (2 * blk, 2 * blk), 0) & (blk - 1)
    c2 = lax.broadcasted_iota(jnp.int32, (2 * blk, 2 * blk), 1)
    uo = jnp.where((r2 > c2) | (c2 >= blk), 1.0, 0.0).astype(BF16)
    c_ref[...] = jnp.zeros_like(c_ref)
    acc_ref[...] = jnp.zeros_like(acc_ref)

    def block(j, diag):
        start = pl.multiple_of(j * blk, blk)
        kj = k_ref[pl.ds(start, blk), :]
        vj = v_ref[pl.ds(start, blk), :]
        if diag:
            mask = (lane + j * blk) < (row + qi * qt)
        pvs = []
        for h in range(2):
            z = _dot_nt(qh[h], kj)
            nz = -z
            lg = jnp.log(1.0 + jnp.exp(jnp.minimum(z, nz)))
            ls = jnp.minimum(nz, 0.0) - lg
            lz = jnp.minimum(z, 0.0) - lg
            if diag:
                ls = jnp.where(mask, ls, 0.0)
            ls_hi = ls.astype(BF16)
            ls_lo = (ls - ls_hi.astype(F32)).astype(BF16)
            s2 = _dot(jnp.concatenate([ls_hi, ls_lo], axis=1), uo)
            c = c_ref[h]
            a = jnp.exp(lz + s2[:, :blk] + c)
            if diag:
                a = jnp.where(mask, a, 0.0)
            pvs.append(_dot(a.astype(BF16), vj))
            c_ref[h] = c + s2[:, blk:]
        acc_ref[...] += jnp.where(head0, pvs[0], pvs[1])

    last = (qi + 1) * kb_per_tile - 1

    def diag_step(it, carry):
        block(last - it, True)
        return carry

    def full_step(it, carry):
        block(qi * kb_per_tile - 1 - it, False)
        return carry

    lax.fori_loop(0, kb_per_tile, diag_step, 0)
    lax.fori_loop(0, qi * kb_per_tile, full_step, 0)
    o_ref[...] = acc_ref[...].astype(o_ref.dtype)


def _sb_attn(q, k, v, batch, seq, qt=512):
    t = q.shape[0]
    nq = seq // qt
    return pl.pallas_call(
        _sb_kernel,
        grid=(batch, SB_WIDTH // LANES, nq),
        in_specs=[pl.BlockSpec((qt, LANES), lambda b, p, i: (b * nq + i, p)),
                  pl.BlockSpec((seq, LANES), lambda b, p, i: (b, p)),
                  pl.BlockSpec((seq, LANES), lambda b, p, i: (b, p))],
        out_specs=pl.BlockSpec((qt, LANES), lambda b, p, i: (b * nq + i, p)),
        out_shape=jax.ShapeDtypeStruct((t, SB_WIDTH), BF16),
        scratch_shapes=[pltpu.VMEM((2, qt, LANES), F32), pltpu.VMEM((qt, LANES), F32)],
        compiler_params=_params("arbitrary", "arbitrary", "arbitrary"),
        name="sb_attn",
    )(q, k, v)


def _gla_kernel(q_ref, k_ref, la_ref, v_ref, g_ref, gn_ref, o_ref):
    ch = GLA_CHUNK
    seq = q_ref.shape[0]
    ri = lax.broadcasted_iota(jnp.int32, (ch, ch), 0)
    ci = lax.broadcasted_iota(jnp.int32, (ch, ch), 1)
    causal = ci <= ri
    tri = jnp.where(causal, 1.0, 0.0).astype(F32)
    lane = lax.broadcasted_iota(jnp.int32, (ch, LANES), 1)
    head0 = lane < 64
    sr = lax.broadcasted_iota(jnp.int32, (2 * LANES, LANES), 0)
    sc = lax.broadcasted_iota(jnp.int32, (2 * LANES, LANES), 1)
    same_head = (sr >= LANES) == (sc >= 64)
    gn = gn_ref[...]

    def chunk(c, state_t):
        r0 = pl.multiple_of(c * ch, ch)
        la = la_ref[pl.ds(r0, ch), :]
        bcum = _dot(tri, la, precision=_HIGH)
        b_last = bcum[ch - 1:ch, :]
        q_in = q_ref[pl.ds(r0, ch), :] * 0.125 * jnp.exp(bcum)
        kc = k_ref[pl.ds(r0, ch), :]
        k_in = (kc * jnp.exp(-bcum)).astype(BF16)
        k_dec = (kc * jnp.exp(b_last - bcum)).astype(BF16)
        vc = v_ref[pl.ds(r0, ch), :]
        qb = q_in.astype(BF16)
        zero = jnp.zeros_like(qb)
        outs = []
        for h in range(2):
            qm = jnp.where(head0, qb, zero) if h == 0 else jnp.where(head0, zero, qb)
            sc_h = jnp.where(causal, _dot_nt(qm, k_in), 0.0)
            outs.append(_dot(sc_h.astype(BF16), vc[:, h * LANES:(h + 1) * LANES]))
        o = jnp.concatenate(outs, axis=1) + _dot_nt(qb, state_t.astype(BF16))
        kv_t = lax.dot_general(vc, k_dec, (((0,), (0,)), ((), ())), preferred_element_type=F32)
        state_t = state_t * jnp.exp(b_last) + jnp.where(same_head, kv_t, 0.0)
        halves = []
        for h in range(2):
            oh = o[:, h * LANES:(h + 1) * LANES]
            halves.append(oh * lax.rsqrt(jnp.mean(oh * oh, axis=-1, keepdims=True) + EPS))
        on = jnp.concatenate(halves, axis=1) * gn
        gg = g_ref[pl.ds(r0, ch), :]
        o_ref[pl.ds(r0, ch), :] = (on * (gg * jax.nn.sigmoid(gg))).astype(o_ref.dtype)
        return state_t

    lax.fori_loop(0, seq // ch, chunk, jnp.zeros((2 * LANES, LANES), F32))


def _gla(qg, kg, la, vg, gg, gla_norm, batch, seq):
    t = qg.shape[0]
    bp = lambda b, p: (b, p)
    return pl.pallas_call(
        _gla_kernel,
        grid=(batch, 2),
        in_specs=[pl.BlockSpec((seq, LANES), bp), pl.BlockSpec((seq, LANES), bp),
                  pl.BlockSpec((seq, LANES), bp), pl.BlockSpec((seq, 2 * LANES), bp),
                  pl.BlockSpec((seq, 2 * LANES), bp), pl.BlockSpec((1, 2 * LANES), lambda b, p: (0, p))],
        out_specs=pl.BlockSpec((seq, 2 * LANES), bp),
        out_shape=jax.ShapeDtypeStruct((t, GLA_VW), BF16),
        compiler_params=_params("arbitrary", "arbitrary"),
        name="gla",
    )(qg, kg, la, vg, gg, gla_norm.reshape(1, -1))


def _ab_out_kernel(x_ref, a_ref, b_ref, wa_ref, wb_ref, o_ref):
    o_ref[...] = x_ref[...] + _dot(a_ref[...], wa_ref[...]) + _dot(b_ref[...], wb_ref[...])


def _ab_out(x2, o_sb, o_g, w_out, tm=512):
    t = x2.shape[0]
    w = w_out.astype(BF16)
    row = lambda i: (i, 0)
    fix = lambda i: (0, 0)
    return pl.pallas_call(
        _ab_out_kernel,
        grid=(t // tm,),
        in_specs=[pl.BlockSpec((tm, D_MODEL), row), pl.BlockSpec((tm, SB_WIDTH), row),
                  pl.BlockSpec((tm, GLA_VW), row), pl.BlockSpec((SB_WIDTH, D_MODEL), fix),
                  pl.BlockSpec((GLA_VW, D_MODEL), fix)],
        out_specs=pl.BlockSpec((tm, D_MODEL), row),
        out_shape=jax.ShapeDtypeStruct((t, D_MODEL), F32),
        compiler_params=_params("arbitrary"),
        name="ab_out",
    )(x2, o_sb, o_g, w[:SB_WIDTH], w[SB_WIDTH:])


def _sgu_kernel(x_ref, g_ref, win_ref, vn_ref, ws_ref, bs_ref, wout_ref, o_ref, y_ref):
    tm = x_ref.shape[0]
    x = x_ref[...]
    h = _rms(x, g_ref[...]).astype(BF16)
    ri = lax.broadcasted_iota(jnp.int32, (SGU_CHUNK, SGU_CHUNK), 0)
    ci = lax.broadcasted_iota(jnp.int32, (SGU_CHUNK, SGU_CHUNK), 1)
    causal = ci <= ri
    v = _gelu(_dot(h, win_ref[:, SGU_WIDTH:]))
    v = (_rms(v, vn_ref[...])).astype(BF16)
    for g in range(SGU_GROUPS):
        cols = slice(g * SGU_GROUP_DIM, (g + 1) * SGU_GROUP_DIM)
        wg = jnp.where(causal, ws_ref[g], 0.0).astype(BF16)
        u_g = _gelu(_dot(h, win_ref[:, cols]))
        bias = bs_ref[g]
        for c in range(tm // SGU_CHUNK):
            rows = slice(c * SGU_CHUNK, (c + 1) * SGU_CHUNK)
            mixed = _dot(wg, v[rows, cols]) + jnp.concatenate([bias, bias], axis=1)
            y_ref[rows, cols] = (u_g[rows] * mixed).astype(BF16)
    o_ref[...] = x + _dot(y_ref[...], wout_ref[...])


def _sgu(x2, g, w_in, v_norm, w_spatial, b_spatial, w_out, tm=256):
    t = x2.shape[0]
    row = lambda i: (i, 0)
    fix = lambda i: (0, 0)
    fix3 = lambda i: (0, 0, 0)
    bias = jnp.broadcast_to(b_spatial[:, :, None], (SGU_GROUPS, SGU_CHUNK, LANES))
    return pl.pallas_call(
        _sgu_kernel,
        grid=(t // tm,),
        in_specs=[pl.BlockSpec((tm, D_MODEL), row), pl.BlockSpec((1, D_MODEL), fix),
                  pl.BlockSpec((D_MODEL, 2 * SGU_WIDTH), fix), pl.BlockSpec((1, SGU_WIDTH), fix),
                  pl.BlockSpec((SGU_GROUPS, SGU_CHUNK, SGU_CHUNK), fix3),
                  pl.BlockSpec((SGU_GROUPS, SGU_CHUNK, LANES), fix3),
                  pl.BlockSpec((SGU_WIDTH, D_MODEL), fix)],
        out_specs=pl.BlockSpec((tm, D_MODEL), row),
        out_shape=jax.ShapeDtypeStruct((t, D_MODEL), F32),
        scratch_shapes=[pltpu.VMEM((tm, SGU_WIDTH), BF16)],
        compiler_params=_params("arbitrary"),
        name="sgu",
    )(x2, g.reshape(1, -1), w_in.astype(BF16), v_norm.reshape(1, -1), w_spatial, bias, w_out.astype(BF16))


def _top16(s, rid):
    big = jnp.int32(2 ** 30)
    vals, idxs = [], []
    for _ in range(PEER_TOPK):
        m = jnp.max(s, axis=0, keepdims=True)
        first = jnp.min(jnp.where(s == m, rid, big), axis=0, keepdims=True)
        vals.append(m)
        idxs.append(first)
        s = jnp.where(rid == first, -jnp.inf, s)
    return jnp.concatenate(vals, axis=0), jnp.concatenate(idxs, axis=0)


def _pair_candidates(s0, s1):
    tm = s0.shape[1]
    sub8 = lax.broadcasted_iota(jnp.int32, (SUBLANES, tm), 0)
    sub16 = lax.broadcasted_iota(jnp.int32, (PEER_TOPK, tm), 0)
    vals = [s0[0:1, :] + s1]
    pos = [sub16]
    for a in range(1, 8):
        v = s0[a:a + 1, :] + s1[0:SUBLANES, :]
        vals.append(jnp.where(sub8 < PEER_TOPK // (a + 1), v, -jnp.inf))
        pos.append(sub8 + a * PEER_TOPK)
    vals.append(s0[SUBLANES:, :] + s1[0:1, :])
    pos.append((sub8 + SUBLANES) * PEER_TOPK)
    return jnp.concatenate(vals, axis=0), jnp.concatenate(pos, axis=0)


def _route_kernel(x_ref, g_ref, wq_ref, sk_ref, h_ref, idx_ref, gate_ref):
    tm = x_ref.shape[0]
    h = _rms(x_ref[...], g_ref[...])
    h_ref[...] = h
    q = _dot(h.astype(BF16), wq_ref[...]).astype(BF16)
    k16 = lax.broadcasted_iota(jnp.int32, (PEER_TOPK, tm), 0)
    key_id = lax.broadcasted_iota(jnp.int32, (N_KEYS, tm), 0)
    idx_rows, gate_rows = [], []
    for hd in range(PEER_HEADS):
        top = []
        for p in range(2):
            j = hd * 2 + p
            s = _dot_nt(sk_ref[j], q[:, j * N_KEYS:(j + 1) * N_KEYS])
            top.append(_top16(s, key_id))
        (s0, i0), (s1, i1) = top
        best, pos = _top16(*_pair_candidates(s0, s1))
        rows = []
        for r in range(PEER_TOPK):
            pa = pos[r:r + 1, :] >> 4
            pb = pos[r:r + 1, :] & 15
            ea = jnp.sum(jnp.where(k16 == pa, i0, 0), axis=0, keepdims=True)
            eb = jnp.sum(jnp.where(k16 == pb, i1, 0), axis=0, keepdims=True)
            rows.append(ea * N_KEYS + eb)
        idx_rows.append(jnp.concatenate(rows, axis=0))
        e = jnp.exp(best - best[0:1, :])
        gate_rows.append(e / jnp.sum(e, axis=0, keepdims=True))
    idx_t = jnp.concatenate(idx_rows, axis=0)
    gate_t = jnp.concatenate(gate_rows, axis=0)
    idx_ref[...] = (idx_t * ROWS_PER_EXPERT).T
    gate_ref[...] = gate_t.T


def _route(x2, g, w_q, subkeys, tm=128):
    t = x2.shape[0]
    row = lambda i: (i, 0)
    fix = lambda i: (0, 0)
    sk = subkeys.reshape(PEER_HEADS * 2, N_KEYS, N_KEYS).astype(BF16)
    return pl.pallas_call(
        _route_kernel,
        grid=(t // tm,),
        in_specs=[pl.BlockSpec((tm, D_MODEL), row), pl.BlockSpec((1, D_MODEL), fix),
                  pl.BlockSpec((D_MODEL, 2 * PEER_HEADS * N_KEYS), fix),
                  pl.BlockSpec((PEER_HEADS * 2, N_KEYS, N_KEYS), lambda i: (0, 0, 0))],
        out_specs=[pl.BlockSpec((tm, D_MODEL), row), pl.BlockSpec((tm, PEER_HK), row),
                   pl.BlockSpec((tm, PEER_HK), row)],
        out_shape=[jax.ShapeDtypeStruct((t, D_MODEL), F32), jax.ShapeDtypeStruct((t, PEER_HK), jnp.int32),
                   jax.ShapeDtypeStruct((t, PEER_HK), F32)],
        compiler_params=_params("arbitrary"),
        name="peer_route",
    )(x2, g.reshape(1, -1), w_q.astype(BF16), sk)


def _pack_table(tab):
    n = tab.shape[0]
    b = lax.bitcast_convert_type(tab.astype(BF16), jnp.uint16).astype(jnp.uint32)
    b = b.reshape(n, ROWS_PER_EXPERT, 2, LANES)
    return (b[:, :, 0, :] | (b[:, :, 1, :] << 16)).reshape(n * ROWS_PER_EXPERT, LANES)


def _slot_masks():
    m = jnp.arange(PEER_HK * SUBLANES)
    chunk_of_m = (m[None, :] % SUBLANES) == jnp.arange(SUBLANES)[:, None]
    slot_of_m = (m[None, :] // SUBLANES) == jnp.arange(PEER_HK)[:, None]
    return chunk_of_m, slot_of_m


def _gather_rows(idx_ref, t, tab_ref, stage_ref):
    for k in range(PEER_HK):
        src = pl.ds(pl.multiple_of(idx_ref[t, k], ROWS_PER_EXPERT), ROWS_PER_EXPERT)
        stage_ref[k * ROWS_PER_EXPERT:(k + 1) * ROWS_PER_EXPERT, :] = tab_ref[src, :]


def _pipelined_tokens(idx_ref, tab_ref, stages, tb, consume):
    n = len(stages)
    for s, stage in enumerate(stages):
        _gather_rows(idx_ref, s, tab_ref, stage)

    def step(i, carry):
        t0 = n * i
        nxt = jnp.minimum(t0 + n, tb - n)
        for s, stage in enumerate(stages):
            consume(t0 + s, stage)
            _gather_rows(idx_ref, nxt + s, tab_ref, stage)
        return carry

    lax.fori_loop(0, tb // n, step, 0)


def _split_bf16(x):
    hi = x.astype(BF16)
    return hi, (x - hi.astype(F32)).astype(BF16)


def _peer_u_kernel(idx_ref, x3_ref, gate_ref, dmask_ref, gsum_ref, tab_ref, w_ref, stage0, stage1, y_ref):
    tb = x3_ref.shape[0]
    dmask = dmask_ref[...]

    def one(t, stage):
        xh, xl = _split_bf16(x3_ref[t])
        xm = jnp.concatenate([xh, xl], axis=0)
        rows = pltpu.bitcast(stage[...], BF16)
        bt = _dot_nt(xm, rows)
        y_ref[pl.ds(t, 1), :] = jnp.sum(bt * dmask, axis=0, keepdims=True)

    _pipelined_tokens(idx_ref, tab_ref, (stage0, stage1), tb, one)
    act = _dot(y_ref[...], gsum_ref[...], precision=_HIGH)
    w_ref[...] = gate_ref[...] * _gelu(act)


def _table_spec():
    return pl.BlockSpec((N_EXPERTS * ROWS_PER_EXPERT, LANES), lambda i: (0, 0), pipeline_mode=pl.Buffered(1))


def _peer_u(idx4, h3, gate, tab, tb=PEER_TB):
    t = idx4.shape[0]
    row = lambda i: (i, 0)
    fix = lambda i: (0, 0)
    chunk_of_m, slot_of_m = _slot_masks()
    dmask = jnp.concatenate([chunk_of_m, chunk_of_m], axis=0).astype(F32)
    gsum = slot_of_m.T.astype(F32)
    nm = PEER_HK * SUBLANES
    return pl.pallas_call(
        _peer_u_kernel,
        grid=(t // tb,),
        in_specs=[pl.BlockSpec((tb, PEER_HK), row, memory_space=pltpu.SMEM),
                  pl.BlockSpec((tb, SUBLANES, LANES), lambda i: (i, 0, 0)),
                  pl.BlockSpec((tb, PEER_HK), row),
                  pl.BlockSpec((2 * SUBLANES, nm), fix),
                  pl.BlockSpec((nm, PEER_HK), fix),
                  _table_spec()],
        out_specs=pl.BlockSpec((tb, PEER_HK), row),
        out_shape=jax.ShapeDtypeStruct((t, PEER_HK), F32),
        scratch_shapes=[pltpu.VMEM((PEER_HK * ROWS_PER_EXPERT, LANES), jnp.uint32),
                        pltpu.VMEM((PEER_HK * ROWS_PER_EXPERT, LANES), jnp.uint32),
                        pltpu.VMEM((tb, nm), F32)],
        compiler_params=_params("arbitrary"),
        name="peer_u",
    )(idx4, h3, gate, dmask, gsum, tab)


def _peer_v_kernel(idx_ref, w_ref, x3_ref, expand_ref, dmask_ref, tab_ref, o_ref, stage0, stage1, w8_ref):
    tb = x3_ref.shape[0]
    wh, wl = _split_bf16(w_ref[...])
    w8_ref[0:tb, :] = _dot(wh, expand_ref[...])
    w8_ref[tb:2 * tb, :] = _dot(wl, expand_ref[...])
    dmask = dmask_ref[...]

    def one(t, stage):
        hi = (w8_ref[pl.ds(t, 1), :] * dmask).astype(BF16)
        lo = (w8_ref[pl.ds(tb + t, 1), :] * dmask).astype(BF16)
        wsel = jnp.concatenate([hi, lo], axis=0)
        rows = pltpu.bitcast(stage[...], BF16)
        o = _dot(wsel, rows)
        o_ref[t] = x3_ref[t] + (o[0:SUBLANES] + o[SUBLANES:])

    _pipelined_tokens(idx_ref, tab_ref, (stage0, stage1), tb, one)


def _peer_v(idx4, w, x3, tab, tb=PEER_TB):
    t = idx4.shape[0]
    row = lambda i: (i, 0)
    fix = lambda i: (0, 0)
    tile = pl.BlockSpec((tb, SUBLANES, LANES), lambda i: (i, 0, 0))
    chunk_of_m, slot_of_m = _slot_masks()
    nm = PEER_HK * SUBLANES
    return pl.pallas_call(
        _peer_v_kernel,
        grid=(t // tb,),
        in_specs=[pl.BlockSpec((tb, PEER_HK), row, memory_space=pltpu.SMEM),
                  pl.BlockSpec((tb, PEER_HK), row),
                  tile,
                  pl.BlockSpec((PEER_HK, nm), fix),
                  pl.BlockSpec((SUBLANES, nm), fix),
                  _table_spec()],
        out_specs=tile,
        out_shape=jax.ShapeDtypeStruct((t, SUBLANES, LANES), F32),
        scratch_shapes=[pltpu.VMEM((PEER_HK * ROWS_PER_EXPERT, LANES), jnp.uint32),
                        pltpu.VMEM((PEER_HK * ROWS_PER_EXPERT, LANES), jnp.uint32),
                        pltpu.VMEM((2 * tb, nm), F32)],
        compiler_params=_params("arbitrary"),
        name="peer_v",
    )(idx4, w, x3, slot_of_m.astype(BF16), chunk_of_m.astype(F32), tab)


def _peer(x2, g, w_q, subkeys, u_tab, v_tab):
    t = x2.shape[0]
    h, idx4, gate = _route(x2, g, w_q, subkeys)
    w = _peer_u(idx4, h.reshape(t, SUBLANES, LANES), gate, _pack_table(u_tab))
    out3 = _peer_v(idx4, w, x2.reshape(t, SUBLANES, LANES), _pack_table(v_tab))
    return out3.reshape(t, D_MODEL)


def _final_kernel(x_ref, g_ref, o_ref):
    o_ref[...] = _rms(x_ref[...], g_ref[...])


def _final_norm(x2, g, tm=1024):
    t = x2.shape[0]
    return pl.pallas_call(
        _final_kernel,
        grid=(t // tm,),
        in_specs=[pl.BlockSpec((tm, D_MODEL), lambda i: (i, 0)), pl.BlockSpec((1, D_MODEL), lambda i: (0, 0))],
        out_specs=pl.BlockSpec((tm, D_MODEL), lambda i: (i, 0)),
        out_shape=jax.ShapeDtypeStruct((t, D_MODEL), F32),
        compiler_params=_params("arbitrary"),
        name="final_norm",
    )(x2, g.reshape(1, -1))


def kernel(x, norm_mix, norm_ffn, ab_w_in, ab_w_alpha2, ab_b_alpha2, ab_gla_norm, ab_w_out, c_w_in, c_v_norm, c_w_spatial, c_b_spatial, c_w_out, peer_w_q, peer_subkeys, peer_u, peer_v, final_norm):
    batch, seq, d = x.shape
    x2 = x.reshape(batch * seq, d)
    depth = norm_mix.shape[0]
    for i in range(depth):
        j = i // 2
        if i % 2 == 0:
            qs, ks, vs, qg, kg, vg, gg, la = _ab_in(x2, norm_mix[i], ab_w_in[j], ab_w_alpha2[j], ab_b_alpha2[j])
            o_sb = _sb_attn(qs, ks, vs, batch, seq)
            o_g = _gla(qg, kg, la, vg, gg, ab_gla_norm[j], batch, seq)
            x2 = _ab_out(x2, o_sb, o_g, ab_w_out[j])
        else:
            x2 = _sgu(x2, norm_mix[i], c_w_in[j], c_v_norm[j], c_w_spatial[j], c_b_spatial[j], c_w_out[j])
        x2 = _peer(x2, norm_ffn[i], peer_w_q[i], peer_subkeys[i], peer_u[i], peer_v[i])
    return _final_norm(x2, final_norm).reshape(batch, seq, d)
```

```python
import functools

import jax
import jax.numpy as jnp
from jax import lax
from jax.experimental import pallas as pl
from jax.experimental.pallas import tpu as pltpu

F32 = jnp.float32
BF16 = jnp.bfloat16
EPS = 1e-6

D_MODEL = 1024
SB_WIDTH = 512
SB_BLOCK = 128
GLA_KW = 256
GLA_VW = 512
GLA_CHUNK = 64
GLA_GATE_NORM = 16.0
SGU_WIDTH = 2048
SGU_GROUPS = 8
SGU_GROUP_DIM = 256
SGU_CHUNK = 128
PEER_HEADS = 8
PEER_TOPK = 16
N_KEYS = 128
N_EXPERTS = N_KEYS * N_KEYS
PEER_HK = PEER_HEADS * PEER_TOPK
PEER_TB = 128
IDX_WINDOW = 16
LANES = 128
SUBLANES = 8
ROWS_PER_EXPERT = D_MODEL // (2 * LANES)
VMEM_LIMIT = 56 * 1024 * 1024

_HIGH = lax.Precision.HIGHEST


def _params(*sem, vmem=VMEM_LIMIT):
    return pltpu.CompilerParams(dimension_semantics=sem, vmem_limit_bytes=vmem)


def _rms(x, g):
    return x * lax.rsqrt(jnp.mean(x * x, axis=-1, keepdims=True) + EPS) * g


def _softplus(z):
    return jnp.maximum(z, 0.0) + jnp.log(1.0 + jnp.exp(-jnp.abs(z)))


def _gelu(x):
    return 0.5 * x * (1.0 + lax.erf(x * 0.7071067811865476))


def _dot_nt(a, b, **kw):
    return lax.dot_general(a, b, (((1,), (1,)), ((), ())), preferred_element_type=F32, **kw)


def _dot(a, b, **kw):
    return jnp.dot(a, b, preferred_element_type=F32, **kw)


def _ab_in_kernel(x_ref, g_ref, w_ref, wa_ref, w2_ref, b2_ref,
                  qs_ref, ks_ref, vs_ref, qg_ref, kg_ref, vg_ref, gg_ref, la_ref):
    h = _rms(x_ref[...], g_ref[...]).astype(BF16)
    qs_ref[...] = (_dot(h, w_ref[:, 0:512]) * 0.125).astype(BF16)
    ks_ref[...] = _dot(h, w_ref[:, 512:1024]).astype(BF16)
    vs_ref[...] = _dot(h, w_ref[:, 1024:1536]).astype(BF16)
    qg_ref[...] = _dot(h, w_ref[:, 1536:1792])
    kg_ref[...] = _dot(h, w_ref[:, 1792:2048])
    vg_ref[...] = _dot(h, w_ref[:, 2048:2560]).astype(BF16)
    gg_ref[...] = _dot(h, w_ref[:, 2560:3072])
    a_lr = _dot(h, wa_ref[...])
    pre = _dot(a_lr.astype(BF16), w2_ref[...]) + b2_ref[...]
    la_ref[...] = -_softplus(-pre) * (1.0 / GLA_GATE_NORM)


def _ab_in(x2, g, w_in, w_alpha2, b_alpha2, tm=512):
    t = x2.shape[0]
    w_main = w_in[:, :3072].astype(BF16)
    wa = jnp.pad(w_in[:, 3072:], ((0, 0), (0, LANES - 16))).astype(BF16)
    w2 = jnp.pad(w_alpha2, ((0, LANES - 16), (0, 0))).astype(BF16)
    row = lambda i: (i, 0)
    fix = lambda i: (0, 0)
    outs = [(512, BF16), (512, BF16), (512, BF16), (256, F32), (256, F32), (512, BF16), (512, F32), (256, F32)]
    return pl.pallas_call(
        _ab_in_kernel,
        grid=(t // tm,),
        in_specs=[pl.BlockSpec((tm, D_MODEL), row), pl.BlockSpec((1, D_MODEL), fix),
                  pl.BlockSpec((D_MODEL, 3072), fix), pl.BlockSpec((D_MODEL, LANES), fix),
                  pl.BlockSpec((LANES, GLA_KW), fix), pl.BlockSpec((1, GLA_KW), fix)],
        out_specs=[pl.BlockSpec((tm, n), row) for n, _ in outs],
        out_shape=[jax.ShapeDtypeStruct((t, n), d) for n, d in outs],
        compiler_params=_params("arbitrary"),
        name="ab_in",
    )(x2, g.reshape(1, -1), w_main, wa, w2, b_alpha2.reshape(1, -1))


def _sb_kernel(q_ref, k_ref, v_ref, o_ref, c_ref, acc_ref, lsb_ref, lz_ref):
    qi = pl.program_id(2)
    blk = SB_BLOCK
    qt = q_ref.shape[0]
    kb_per_tile = qt // blk
    lane = lax.broadcasted_iota(jnp.int32, (qt, blk), 1)
    row = lax.broadcasted_iota(jnp.int32, (qt, blk), 0)
    head0 = lane < 64
    q = q_ref[...]
    zero = jnp.zeros_like(q)
    qh = (jnp.where(head0, q, zero), jnp.where(head0, zero, q))
    r2 = lax.broadcasted_iota(jnp.int32, (2 * blk, 2 * blk), 0) & (blk - 1)
    c2 = lax.broadcasted_iota(jnp.int32, (2 * blk, 2 * blk), 1)
    uo = jnp.where((r2 > c2) | (c2 >= blk), 1.0, 0.0).astype(BF16)
    c_ref[...] = jnp.zeros_like(c_ref)
    acc_ref[...] = jnp.zeros_like(acc_ref)

    def logits(j, diag, slot):
        kj = k_ref[pl.ds(pl.multiple_of(j * blk, blk), blk), :]
        for h in range(2):
            z = _dot_nt(qh[h], kj)
            nz = -z
            lg = jnp.log(1.0 + jnp.exp(jnp.minimum(z, nz)))
            ls = jnp.minimum(nz, 0.0) - lg
            if diag:
                ls = jnp.where((lane + j * blk) < (row + qi * qt), ls, 0.0)
            ls_hi = ls.astype(BF16)
            ls_lo = (ls - ls_hi.astype(F32)).astype(BF16)
            lsb_ref[slot, h] = jnp.concatenate([ls_hi, ls_lo], axis=1)
            lz_ref[slot, h] = jnp.minimum(z, 0.0) - lg

    def attend(j, diag, slot):
        vj = v_ref[pl.ds(pl.multiple_of(j * blk, blk), blk), :]
        pvs = []
        for h in range(2):
            s2 = _dot(lsb_ref[slot, h], uo)
            c = c_ref[h]
            a = jnp.exp(lz_ref[slot, h] + s2[:, :blk] + c)
            if diag:
                a = jnp.where((lane + j * blk) < (row + qi * qt), a, 0.0)
            pvs.append(_dot(a.astype(BF16), vj))
            c_ref[h] = c + s2[:, blk:]
        acc_ref[...] += jnp.where(head0, pvs[0], pvs[1])

    last = (qi + 1) * kb_per_tile - 1
    n_full = qi * kb_per_tile
    logits(last, True, 0)
    for it in range(kb_per_tile):
        attend(last - it, True, it % 2)
        logits(jnp.maximum(last - it - 1, 0), it < kb_per_tile - 1, (it + 1) % 2)

    def full_step(it, carry):
        j = n_full - 1 - it
        slot = (it + kb_per_tile) % 2
        attend(j, False, slot)
        logits(jnp.maximum(j - 1, 0), False, 1 - slot)
        return carry

    lax.fori_loop(0, n_full, full_step, 0)
    o_ref[...] = acc_ref[...].astype(o_ref.dtype)


def _sb_attn(q, k, v, batch, seq, qt=512):
    t = q.shape[0]
    nq = seq // qt
    return pl.pallas_call(
        _sb_kernel,
        grid=(batch, SB_WIDTH // LANES, nq),
        in_specs=[pl.BlockSpec((qt, LANES), lambda b, p, i: (b * nq + i, p)),
                  pl.BlockSpec((seq, LANES), lambda b, p, i: (b, p)),
                  pl.BlockSpec((seq, LANES), lambda b, p, i: (b, p))],
        out_specs=pl.BlockSpec((qt, LANES), lambda b, p, i: (b * nq + i, p)),
        out_shape=jax.ShapeDtypeStruct((t, SB_WIDTH), BF16),
        scratch_shapes=[pltpu.VMEM((2, qt, LANES), F32), pltpu.VMEM((qt, LANES), F32),
                        pltpu.VMEM((2, 2, qt, 2 * LANES), BF16), pltpu.VMEM((2, 2, qt, LANES), F32)],
        compiler_params=_params("arbitrary", "arbitrary", "arbitrary"),
        name="sb_attn",
    )(q, k, v)


def _gla_kernel(q_ref, k_ref, la_ref, v_ref, g_ref, gn_ref, o_ref):
    ch = GLA_CHUNK
    seq = q_ref.shape[0]
    ri = lax.broadcasted_iota(jnp.int32, (ch, ch), 0)
    ci = lax.broadcasted_iota(jnp.int32, (ch, ch), 1)
    causal = ci <= ri
    tri = jnp.where(causal, 1.0, 0.0).astype(F32)
    lane = lax.broadcasted_iota(jnp.int32, (ch, LANES), 1)
    head0 = lane < 64
    sr = lax.broadcasted_iota(jnp.int32, (2 * LANES, LANES), 0)
    sc = lax.broadcasted_iota(jnp.int32, (2 * LANES, LANES), 1)
    same_head = (sr >= LANES) == (sc >= 64)
    n_pairs = q_ref.shape[1] // LANES

    def pair_chunk(p, r0, state_t):
        kcols = slice(p * LANES, (p + 1) * LANES)
        vcols = slice(2 * p * LANES, 2 * (p + 1) * LANES)
        la = la_ref[pl.ds(r0, ch), kcols]
        bcum = _dot(tri, la, precision=_HIGH)
        b_last = bcum[ch - 1:ch, :]
        q_in = q_ref[pl.ds(r0, ch), kcols] * 0.125 * jnp.exp(bcum)
        kc = k_ref[pl.ds(r0, ch), kcols]
        k_in = (kc * jnp.exp(-bcum)).astype(BF16)
        k_dec = (kc * jnp.exp(b_last - bcum)).astype(BF16)
        vc = v_ref[pl.ds(r0, ch), vcols]
        qb = q_in.astype(BF16)
        zero = jnp.zeros_like(qb)
        outs = []
        for h in range(2):
            qm = jnp.where(head0, qb, zero) if h == 0 else jnp.where(head0, zero, qb)
            sc_h = jnp.where(causal, _dot_nt(qm, k_in), 0.0)
            outs.append(_dot(sc_h.astype(BF16), vc[:, h * LANES:(h + 1) * LANES]))
        o = jnp.concatenate(outs, axis=1) + _dot_nt(qb, state_t.astype(BF16))
        kv_t = lax.dot_general(vc, k_dec, (((0,), (0,)), ((), ())), preferred_element_type=F32)
        state_t = state_t * jnp.exp(b_last) + jnp.where(same_head, kv_t, 0.0)
        halves = []
        for h in range(2):
            oh = o[:, h * LANES:(h + 1) * LANES]
            halves.append(oh * lax.rsqrt(jnp.mean(oh * oh, axis=-1, keepdims=True) + EPS))
        on = jnp.concatenate(halves, axis=1) * gn_ref[:, vcols]
        gg = g_ref[pl.ds(r0, ch), vcols]
        o_ref[pl.ds(r0, ch), vcols] = (on * (gg * jax.nn.sigmoid(gg))).astype(o_ref.dtype)
        return state_t

    def chunk(c, states):
        r0 = pl.multiple_of(c * ch, ch)
        return tuple(pair_chunk(p, r0, states[p]) for p in range(n_pairs))

    lax.fori_loop(0, seq // ch, chunk, tuple(jnp.zeros((2 * LANES, LANES), F32) for _ in range(n_pairs)))


def _gla(qg, kg, la, vg, gg, gla_norm, batch, seq):
    t = qg.shape[0]
    bp = lambda b: (b, 0)
    return pl.pallas_call(
        _gla_kernel,
        grid=(batch,),
        in_specs=[pl.BlockSpec((seq, GLA_KW), bp), pl.BlockSpec((seq, GLA_KW), bp),
                  pl.BlockSpec((seq, GLA_KW), bp), pl.BlockSpec((seq, GLA_VW), bp),
                  pl.BlockSpec((seq, GLA_VW), bp), pl.BlockSpec((1, GLA_VW), lambda b: (0, 0))],
        out_specs=pl.BlockSpec((seq, GLA_VW), bp),
        out_shape=jax.ShapeDtypeStruct((t, GLA_VW), BF16),
        compiler_params=_params("arbitrary"),
        name="gla",
    )(qg, kg, la, vg, gg, gla_norm.reshape(1, -1))


def _ab_out_kernel(x_ref, a_ref, b_ref, wa_ref, wb_ref, o_ref):
    o_ref[...] = x_ref[...] + _dot(a_ref[...], wa_ref[...]) + _dot(b_ref[...], wb_ref[...])


def _ab_out(x2, o_sb, o_g, w_out, tm=512):
    t = x2.shape[0]
    w = w_out.astype(BF16)
    row = lambda i: (i, 0)
    fix = lambda i: (0, 0)
    return pl.pallas_call(
        _ab_out_kernel,
        grid=(t // tm,),
        in_specs=[pl.BlockSpec((tm, D_MODEL), row), pl.BlockSpec((tm, SB_WIDTH), row),
                  pl.BlockSpec((tm, GLA_VW), row), pl.BlockSpec((SB_WIDTH, D_MODEL), fix),
                  pl.BlockSpec((GLA_VW, D_MODEL), fix)],
        out_specs=pl.BlockSpec((tm, D_MODEL), row),
        out_shape=jax.ShapeDtypeStruct((t, D_MODEL), F32),
        compiler_params=_params("arbitrary"),
        name="ab_out",
    )(x2, o_sb, o_g, w[:SB_WIDTH], w[SB_WIDTH:])


def _sgu_kernel(x_ref, g_ref, win_ref, vn_ref, ws_ref, bs_ref, wout_ref, o_ref, y_ref):
    tm = x_ref.shape[0]
    x = x_ref[...]
    h = _rms(x, g_ref[...]).astype(BF16)
    ri = lax.broadcasted_iota(jnp.int32, (SGU_CHUNK, SGU_CHUNK), 0)
    ci = lax.broadcasted_iota(jnp.int32, (SGU_CHUNK, SGU_CHUNK), 1)
    causal = ci <= ri
    v = _gelu(_dot(h, win_ref[:, SGU_WIDTH:]))
    v = (_rms(v, vn_ref[...])).astype(BF16)
    for g in range(SGU_GROUPS):
        cols = slice(g * SGU_GROUP_DIM, (g + 1) * SGU_GROUP_DIM)
        wg = jnp.where(causal, ws_ref[g], 0.0).astype(BF16)
        u_g = _gelu(_dot(h, win_ref[:, cols]))
        bias = bs_ref[g]
        for c in range(tm // SGU_CHUNK):
            rows = slice(c * SGU_CHUNK, (c + 1) * SGU_CHUNK)
            mixed = _dot(wg, v[rows, cols]) + jnp.concatenate([bias, bias], axis=1)
            y_ref[rows, cols] = (u_g[rows] * mixed).astype(BF16)
    o_ref[...] = x + _dot(y_ref[...], wout_ref[...])


def _sgu(x2, g, w_in, v_norm, w_spatial, b_spatial, w_out, tm=256):
    t = x2.shape[0]
    row = lambda i: (i, 0)
    fix = lambda i: (0, 0)
    fix3 = lambda i: (0, 0, 0)
    bias = jnp.broadcast_to(b_spatial[:, :, None], (SGU_GROUPS, SGU_CHUNK, LANES))
    return pl.pallas_call(
        _sgu_kernel,
        grid=(t // tm,),
        in_specs=[pl.BlockSpec((tm, D_MODEL), row), pl.BlockSpec((1, D_MODEL), fix),
                  pl.BlockSpec((D_MODEL, 2 * SGU_WIDTH), fix), pl.BlockSpec((1, SGU_WIDTH), fix),
                  pl.BlockSpec((SGU_GROUPS, SGU_CHUNK, SGU_CHUNK), fix3),
                  pl.BlockSpec((SGU_GROUPS, SGU_CHUNK, LANES), fix3),
                  pl.BlockSpec((SGU_WIDTH, D_MODEL), fix)],
        out_specs=pl.BlockSpec((tm, D_MODEL), row),
        out_shape=jax.ShapeDtypeStruct((t, D_MODEL), F32),
        scratch_shapes=[pltpu.VMEM((tm, SGU_WIDTH), BF16)],
        compiler_params=_params("arbitrary"),
        name="sgu",
    )(x2, g.reshape(1, -1), w_in.astype(BF16), v_norm.reshape(1, -1), w_spatial, bias, w_out.astype(BF16))


def _top16(s, rid):
    big = jnp.int32(2 ** 30)
    vals, idxs = [], []
    for _ in range(PEER_TOPK):
        m = jnp.max(s, axis=0, keepdims=True)
        first = jnp.min(jnp.where(s == m, rid, big), axis=0, keepdims=True)
        vals.append(m)
        idxs.append(first)
        s = jnp.where(rid == first, -jnp.inf, s)
    return jnp.concatenate(vals, axis=0), jnp.concatenate(idxs, axis=0)


def _pair_candidates(s0, s1):
    tm = s0.shape[1]
    sub8 = lax.broadcasted_iota(jnp.int32, (SUBLANES, tm), 0)
    sub16 = lax.broadcasted_iota(jnp.int32, (PEER_TOPK, tm), 0)
    vals = [s0[0:1, :] + s1]
    pos = [sub16]
    for a in range(1, 8):
        v = s0[a:a + 1, :] + s1[0:SUBLANES, :]
        vals.append(jnp.where(sub8 < PEER_TOPK // (a + 1), v, -jnp.inf))
        pos.append(sub8 + a * PEER_TOPK)
    vals.append(s0[SUBLANES:, :] + s1[0:1, :])
    pos.append((sub8 + SUBLANES) * PEER_TOPK)
    return jnp.concatenate(vals, axis=0), jnp.concatenate(pos, axis=0)


def _route_kernel(x_ref, g_ref, wq_ref, sk_ref, h_ref, idx_ref, gate_ref):
    tm = x_ref.shape[0]
    h = _rms(x_ref[...], g_ref[...])
    h_ref[...] = h
    q = _dot(h.astype(BF16), wq_ref[...]).astype(BF16)
    k16 = lax.broadcasted_iota(jnp.int32, (PEER_TOPK, tm), 0)
    key_id = lax.broadcasted_iota(jnp.int32, (N_KEYS, tm), 0)
    idx_rows, gate_rows = [], []
    for hd in range(PEER_HEADS):
        top = []
        for p in range(2):
            j = hd * 2 + p
            s = _dot_nt(sk_ref[j], q[:, j * N_KEYS:(j + 1) * N_KEYS])
            top.append(_top16(s, key_id))
        (s0, i0), (s1, i1) = top
        best, pos = _top16(*_pair_candidates(s0, s1))
        rows = []
        for r in range(PEER_TOPK):
            pa = pos[r:r + 1, :] >> 4
            pb = pos[r:r + 1, :] & 15
            ea = jnp.sum(jnp.where(k16 == pa, i0, 0), axis=0, keepdims=True)
            eb = jnp.sum(jnp.where(k16 == pb, i1, 0), axis=0, keepdims=True)
            rows.append(ea * N_KEYS + eb)
        idx_rows.append(jnp.concatenate(rows, axis=0))
        e = jnp.exp(best - best[0:1, :])
        gate_rows.append(e / jnp.sum(e, axis=0, keepdims=True))
    idx_t = jnp.concatenate(idx_rows, axis=0)
    gate_t = jnp.concatenate(gate_rows, axis=0)
    idx_ref[...] = (idx_t * ROWS_PER_EXPERT).T
    gate_ref[...] = gate_t.T


def _route(x2, g, w_q, subkeys, tm=128):
    t = x2.shape[0]
    row = lambda i: (i, 0)
    fix = lambda i: (0, 0)
    sk = subkeys.reshape(PEER_HEADS * 2, N_KEYS, N_KEYS).astype(BF16)
    return pl.pallas_call(
        _route_kernel,
        grid=(t // tm,),
        in_specs=[pl.BlockSpec((tm, D_MODEL), row), pl.BlockSpec((1, D_MODEL), fix),
                  pl.BlockSpec((D_MODEL, 2 * PEER_HEADS * N_KEYS), fix),
                  pl.BlockSpec((PEER_HEADS * 2, N_KEYS, N_KEYS), lambda i: (0, 0, 0))],
        out_specs=[pl.BlockSpec((tm, D_MODEL), row), pl.BlockSpec((tm, PEER_HK), row),
                   pl.BlockSpec((tm, PEER_HK), row)],
        out_shape=[jax.ShapeDtypeStruct((t, D_MODEL), F32), jax.ShapeDtypeStruct((t, PEER_HK), jnp.int32),
                   jax.ShapeDtypeStruct((t, PEER_HK), F32)],
        compiler_params=_params("arbitrary"),
        name="peer_route",
    )(x2, g.reshape(1, -1), w_q.astype(BF16), sk)


def _pack_table(tab):
    n = tab.shape[0]
    b = lax.bitcast_convert_type(tab.astype(BF16), jnp.uint16).astype(jnp.uint32)
    b = b.reshape(n, ROWS_PER_EXPERT, 2, LANES)
    return (b[:, :, 0, :] | (b[:, :, 1, :] << 16)).reshape(n * ROWS_PER_EXPERT, LANES)


def _slot_masks():
    m = jnp.arange(PEER_HK * SUBLANES)
    chunk_of_m = (m[None, :] % SUBLANES) == jnp.arange(SUBLANES)[:, None]
    slot_of_m = (m[None, :] // SUBLANES) == jnp.arange(PEER_HK)[:, None]
    return chunk_of_m, slot_of_m


def _gather_rows(iwin_ref, slot, tab_ref, stage_ref):
    for k in range(PEER_HK):
        src = pl.ds(pl.multiple_of(iwin_ref[slot, k], ROWS_PER_EXPERT), ROWS_PER_EXPERT)
        stage_ref[k * ROWS_PER_EXPERT:(k + 1) * ROWS_PER_EXPERT, :] = tab_ref[src, :]


def _pipelined_tokens(idx_ref, tab_ref, stages, iwin_ref, sems, tb, consume):
    half = IDX_WINDOW // 2
    n_stage = len(stages)

    def idx_copy(first_token, g):
        src = pl.multiple_of(jnp.minimum(first_token, tb - half), half)
        return pltpu.make_async_copy(idx_ref.at[pl.ds(src, half)], iwin_ref.at[pl.ds(g * half, half)], sems.at[g])

    idx_copy(0, 0).start()
    idx_copy(half, 1).start()
    idx_copy(0, 0).wait()
    for s in range(n_stage):
        _gather_rows(iwin_ref, s, tab_ref, stages[s])

    def window(n, carry):
        base = IDX_WINDOW * n
        for tau in range(IDX_WINDOW):
            stage = stages[tau % n_stage]
            consume(base + tau, stage)
            if tau == half - n_stage:
                idx_copy(0, 1).wait()
            if tau == IDX_WINDOW - n_stage:
                idx_copy(0, 0).wait()
            _gather_rows(iwin_ref, (tau + n_stage) % IDX_WINDOW, tab_ref, stage)
            if tau == half - n_stage - 1:
                idx_copy(base + IDX_WINDOW, 0).start()
            if tau == IDX_WINDOW - n_stage - 1:
                idx_copy(base + IDX_WINDOW + half, 1).start()
        return carry

    lax.fori_loop(0, tb // IDX_WINDOW, window, 0)
    idx_copy(0, 1).wait()


def _split_bf16(x):
    hi = x.astype(BF16)
    return hi, (x - hi.astype(F32)).astype(BF16)


def _peer_u_kernel(idx_ref, x3_ref, gate_ref, dmask_ref, gsum_ref, tab_ref, w_ref,
                   stage0, stage1, iwin_ref, sems, y_ref):
    tb = x3_ref.shape[0]
    dmask = dmask_ref[...]

    def one(t, stage):
        xh, xl = _split_bf16(x3_ref[t])
        xm = jnp.concatenate([xh, xl], axis=0)
        rows = pltpu.bitcast(stage[...], BF16)
        bt = _dot_nt(xm, rows)
        y_ref[pl.ds(t, 1), :] = jnp.sum(bt * dmask, axis=0, keepdims=True)

    _pipelined_tokens(idx_ref, tab_ref, (stage0, stage1), iwin_ref, sems, tb, one)
    act = _dot(y_ref[...], gsum_ref[...], precision=_HIGH)
    w_ref[...] = gate_ref[...] * _gelu(act)


def _table_spec():
    return pl.BlockSpec((N_EXPERTS * ROWS_PER_EXPERT, LANES), lambda i: (0, 0), pipeline_mode=pl.Buffered(1))


def _peer_u(idx4, h3, gate, tab, tb=PEER_TB):
    t = idx4.shape[0]
    row = lambda i: (i, 0)
    fix = lambda i: (0, 0)
    chunk_of_m, slot_of_m = _slot_masks()
    dmask = jnp.concatenate([chunk_of_m, chunk_of_m], axis=0).astype(F32)
    gsum = slot_of_m.T.astype(F32)
    nm = PEER_HK * SUBLANES
    return pl.pallas_call(
        _peer_u_kernel,
        grid=(t // tb,),
        in_specs=[pl.BlockSpec((tb, PEER_HK), row),
                  pl.BlockSpec((tb, SUBLANES, LANES), lambda i: (i, 0, 0)),
                  pl.BlockSpec((tb, PEER_HK), row),
                  pl.BlockSpec((2 * SUBLANES, nm), fix),
                  pl.BlockSpec((nm, PEER_HK), fix),
                  _table_spec()],
        out_specs=pl.BlockSpec((tb, PEER_HK), row),
        out_shape=jax.ShapeDtypeStruct((t, PEER_HK), F32),
        scratch_shapes=[pltpu.VMEM((PEER_HK * ROWS_PER_EXPERT, LANES), jnp.uint32),
                        pltpu.VMEM((PEER_HK * ROWS_PER_EXPERT, LANES), jnp.uint32),
                        pltpu.SMEM((IDX_WINDOW, PEER_HK), jnp.int32),
                        pltpu.SemaphoreType.DMA((2,)),
                        pltpu.VMEM((tb, nm), F32)],
        compiler_params=_params("arbitrary"),
        name="peer_u",
    )(idx4, h3, gate, dmask, gsum, tab)


def _peer_v_kernel(idx_ref, w_ref, x3_ref, expand_ref, dmask_ref, tab_ref, o_ref,
                   stage0, stage1, iwin_ref, sems, w8_ref):
    tb = x3_ref.shape[0]
    wh, wl = _split_bf16(w_ref[...])
    w8_ref[0:tb, :] = _dot(wh, expand_ref[...])
    w8_ref[tb:2 * tb, :] = _dot(wl, expand_ref[...])
    dmask = dmask_ref[...]

    def one(t, stage):
        hi = (w8_ref[pl.ds(t, 1), :] * dmask).astype(BF16)
        lo = (w8_ref[pl.ds(tb + t, 1), :] * dmask).astype(BF16)
        wsel = jnp.concatenate([hi, lo], axis=0)
        rows = pltpu.bitcast(stage[...], BF16)
        o = _dot(wsel, rows)
        o_ref[t] = x3_ref[t] + (o[0:SUBLANES] + o[SUBLANES:])

    _pipelined_tokens(idx_ref, tab_ref, (stage0, stage1), iwin_ref, sems, tb, one)


def _peer_v(idx4, w, x3, tab, tb=PEER_TB):
    t = idx4.shape[0]
    row = lambda i: (i, 0)
    fix = lambda i: (0, 0)
    tile = pl.BlockSpec((tb, SUBLANES, LANES), lambda i: (i, 0, 0))
    chunk_of_m, slot_of_m = _slot_masks()
    nm = PEER_HK * SUBLANES
    return pl.pallas_call(
        _peer_v_kernel,
        grid=(t // tb,),
        in_specs=[pl.BlockSpec((tb, PEER_HK), row),
                  pl.BlockSpec((tb, PEER_HK), row),
                  tile,
                  pl.BlockSpec((PEER_HK, nm), fix),
                  pl.BlockSpec((SUBLANES, nm), fix),
                  _table_spec()],
        out_specs=tile,
        out_shape=jax.ShapeDtypeStruct((t, SUBLANES, LANES), F32),
        scratch_shapes=[pltpu.VMEM((PEER_HK * ROWS_PER_EXPERT, LANES), jnp.uint32),
                        pltpu.VMEM((PEER_HK * ROWS_PER_EXPERT, LANES), jnp.uint32),
                        pltpu.SMEM((IDX_WINDOW, PEER_HK), jnp.int32),
                        pltpu.SemaphoreType.DMA((2,)),
                        pltpu.VMEM((2 * tb, nm), F32)],
        compiler_params=_params("arbitrary"),
        name="peer_v",
    )(idx4, w, x3, slot_of_m.astype(BF16), chunk_of_m.astype(F32), tab)


def _peer(x2, g, w_q, subkeys, u_tab, v_tab):
    t = x2.shape[0]
    h, idx4, gate = _route(x2, g, w_q, subkeys)
    w = _peer_u(idx4, h.reshape(t, SUBLANES, LANES), gate, _pack_table(u_tab))
    out3 = _peer_v(idx4, w, x2.reshape(t, SUBLANES, LANES), _pack_table(v_tab))
    return out3.reshape(t, D_MODEL)


def _final_kernel(x_ref, g_ref, o_ref):
    o_ref[...] = _rms(x_ref[...], g_ref[...])


def _final_norm(x2, g, tm=1024):
    t = x2.shape[0]
    return pl.pallas_call(
        _final_kernel,
        grid=(t // tm,),
        in_specs=[pl.BlockSpec((tm, D_MODEL), lambda i: (i, 0)), pl.BlockSpec((1, D_MODEL), lambda i: (0, 0))],
        out_specs=pl.BlockSpec((tm, D_MODEL), lambda i: (i, 0)),
        out_shape=jax.ShapeDtypeStruct((t, D_MODEL), F32),
        compiler_params=_params("arbitrary"),
        name="final_norm",
    )(x2, g.reshape(1, -1))


def kernel(x, norm_mix, norm_ffn, ab_w_in, ab_w_alpha2, ab_b_alpha2, ab_gla_norm, ab_w_out, c_w_in, c_v_norm, c_w_spatial, c_b_spatial, c_w_out, peer_w_q, peer_subkeys, peer_u, peer_v, final_norm):
    batch, seq, d = x.shape
    x2 = x.reshape(batch * seq, d)
    depth = norm_mix.shape[0]
    for i in range(depth):
        j = i // 2
        if i % 2 == 0:
            qs, ks, vs, qg, kg, vg, gg, la = _ab_in(x2, norm_mix[i], ab_w_in[j], ab_w_alpha2[j], ab_b_alpha2[j])
            o_sb = _sb_attn(qs, ks, vs, batch, seq)
            o_g = _gla(qg, kg, la, vg, gg, ab_gla_norm[j], batch, seq)
            x2 = _ab_out(x2, o_sb, o_g, ab_w_out[j])
        else:
            x2 = _sgu(x2, norm_mix[i], c_w_in[j], c_v_norm[j], c_w_spatial[j], c_b_spatial[j], c_w_out[j])
        x2 = _peer(x2, norm_ffn[i], peer_w_q[i], peer_subkeys[i], peer_u[i], peer_v[i])
    return _final_norm(x2, final_norm).reshape(batch, seq, d)
```

```python
import functools

import jax
import jax.numpy as jnp
from jax import lax
from jax.experimental import pallas as pl
from jax.experimental.pallas import tpu as pltpu

F32 = jnp.float32
BF16 = jnp.bfloat16
EPS = 1e-6

D_MODEL = 1024
SB_WIDTH = 512
SB_BLOCK = 128
GLA_KW = 256
GLA_VW = 512
GLA_CHUNK = 64
GLA_GATE_NORM = 16.0
SGU_WIDTH = 2048
SGU_GROUPS = 8
SGU_GROUP_DIM = 256
SGU_CHUNK = 128
PEER_HEADS = 8
PEER_TOPK = 16
N_KEYS = 128
N_EXPERTS = N_KEYS * N_KEYS
PEER_HK = PEER_HEADS * PEER_TOPK
PEER_TB = 128
IDX_WINDOW = 32
IDX_GROUP = 8
LANES = 128
SUBLANES = 8
ROWS_PER_EXPERT = D_MODEL // (2 * LANES)
VMEM_LIMIT = 56 * 1024 * 1024

_HIGH = lax.Precision.HIGHEST


def _params(*sem, vmem=VMEM_LIMIT):
    return pltpu.CompilerParams(dimension_semantics=sem, vmem_limit_bytes=vmem)


def _rms(x, g):
    return x * lax.rsqrt(jnp.mean(x * x, axis=-1, keepdims=True) + EPS) * g


def _softplus(z):
    return jnp.maximum(z, 0.0) + jnp.log(1.0 + jnp.exp(-jnp.abs(z)))


def _gelu(x):
    return 0.5 * x * (1.0 + lax.erf(x * 0.7071067811865476))


def _dot_nt(a, b, **kw):
    return lax.dot_general(a, b, (((1,), (1,)), ((), ())), preferred_element_type=F32, **kw)


def _dot(a, b, **kw):
    return jnp.dot(a, b, preferred_element_type=F32, **kw)


def _ab_in_kernel(x_ref, g_ref, w_ref, wa_ref, w2_ref, b2_ref,
                  qs_ref, ks_ref, vs_ref, qg_ref, kg_ref, vg_ref, gg_ref, la_ref):
    h = _rms(x_ref[...], g_ref[...]).astype(BF16)
    qs_ref[...] = (_dot(h, w_ref[:, 0:512]) * 0.125).astype(BF16)
    ks_ref[...] = _dot(h, w_ref[:, 512:1024]).astype(BF16)
    vs_ref[...] = _dot(h, w_ref[:, 1024:1536]).astype(BF16)
    qg_ref[...] = _dot(h, w_ref[:, 1536:1792])
    kg_ref[...] = _dot(h, w_ref[:, 1792:2048])
    vg_ref[...] = _dot(h, w_ref[:, 2048:2560]).astype(BF16)
    gg_ref[...] = _dot(h, w_ref[:, 2560:3072])
    a_lr = _dot(h, wa_ref[...])
    pre = _dot(a_lr.astype(BF16), w2_ref[...]) + b2_ref[...]
    la_ref[...] = -_softplus(-pre) * (1.0 / GLA_GATE_NORM)


def _ab_in(x2, g, w_in, w_alpha2, b_alpha2, tm=512):
    t = x2.shape[0]
    w_main = w_in[:, :3072].astype(BF16)
    wa = jnp.pad(w_in[:, 3072:], ((0, 0), (0, LANES - 16))).astype(BF16)
    w2 = jnp.pad(w_alpha2, ((0, LANES - 16), (0, 0))).astype(BF16)
    row = lambda i: (i, 0)
    fix = lambda i: (0, 0)
    outs = [(512, BF16), (512, BF16), (512, BF16), (256, F32), (256, F32), (512, BF16), (512, F32), (256, F32)]
    return pl.pallas_call(
        _ab_in_kernel,
        grid=(t // tm,),
        in_specs=[pl.BlockSpec((tm, D_MODEL), row), pl.BlockSpec((1, D_MODEL), fix),
                  pl.BlockSpec((D_MODEL, 3072), fix), pl.BlockSpec((D_MODEL, LANES), fix),
                  pl.BlockSpec((LANES, GLA_KW), fix), pl.BlockSpec((1, GLA_KW), fix)],
        out_specs=[pl.BlockSpec((tm, n), row) for n, _ in outs],
        out_shape=[jax.ShapeDtypeStruct((t, n), d) for n, d in outs],
        compiler_params=_params("arbitrary"),
        name="ab_in",
    )(x2, g.reshape(1, -1), w_main, wa, w2, b_alpha2.reshape(1, -1))


def _sb_kernel(q_ref, k_ref, v_ref, o_ref, c_ref, acc_ref, lsb_ref, lz_ref):
    qi = pl.program_id(2)
    blk = SB_BLOCK
    qt = q_ref.shape[0]
    kb_per_tile = qt // blk
    lane = lax.broadcasted_iota(jnp.int32, (qt, blk), 1)
    row = lax.broadcasted_iota(jnp.int32, (qt, blk), 0)
    head0 = lane < 64
    q = q_ref[...]
    zero = jnp.zeros_like(q)
    qh = (jnp.where(head0, q, zero), jnp.where(head0, zero, q))
    r2 = lax.broadcasted_iota(jnp.int32, (2 * blk, 2 * blk), 0) & (blk - 1)
    c2 = lax.broadcasted_iota(jnp.int32, (2 * blk, 2 * blk), 1)
    uo = jnp.where((r2 > c2) | (c2 >= blk), 1.0, 0.0).astype(BF16)
    c_ref[...] = jnp.zeros_like(c_ref)
    acc_ref[...] = jnp.zeros_like(acc_ref)

    def logits(j, diag, slot):
        kj = k_ref[pl.ds(pl.multiple_of(j * blk, blk), blk), :]
        for h in range(2):
            z = _dot_nt(qh[h], kj)
            nz = -z
            lg = jnp.log(1.0 + jnp.exp(jnp.minimum(z, nz)))
            ls = jnp.minimum(nz, 0.0) - lg
            if diag:
                ls = jnp.where((lane + j * blk) < (row + qi * qt), ls, 0.0)
            ls_hi = ls.astype(BF16)
            ls_lo = (ls - ls_hi.astype(F32)).astype(BF16)
            lsb_ref[slot, h] = jnp.concatenate([ls_hi, ls_lo], axis=1)
            lz_ref[slot, h] = jnp.minimum(z, 0.0) - lg

    def attend(j, diag, slot):
        vj = v_ref[pl.ds(pl.multiple_of(j * blk, blk), blk), :]
        pvs = []
        for h in range(2):
            s2 = _dot(lsb_ref[slot, h], uo)
            c = c_ref[h]
            a = jnp.exp(lz_ref[slot, h] + s2[:, :blk] + c)
            if diag:
                a = jnp.where((lane + j * blk) < (row + qi * qt), a, 0.0)
            pvs.append(_dot(a.astype(BF16), vj))
            c_ref[h] = c + s2[:, blk:]
        acc_ref[...] += jnp.where(head0, pvs[0], pvs[1])

    last = (qi + 1) * kb_per_tile - 1
    n_full = qi * kb_per_tile
    logits(last, True, 0)
    for it in range(kb_per_tile):
        attend(last - it, True, it % 2)
        logits(jnp.maximum(last - it - 1, 0), it < kb_per_tile - 1, (it + 1) % 2)

    def full_step(it, carry):
        j = n_full - 1 - it
        slot = (it + kb_per_tile) % 2
        attend(j, False, slot)
        logits(jnp.maximum(j - 1, 0), False, 1 - slot)
        return carry

    lax.fori_loop(0, n_full, full_step, 0)
    o_ref[...] = acc_ref[...].astype(o_ref.dtype)


def _sb_attn(q, k, v, batch, seq, qt=512):
    t = q.shape[0]
    nq = seq // qt
    return pl.pallas_call(
        _sb_kernel,
        grid=(batch, SB_WIDTH // LANES, nq),
        in_specs=[pl.BlockSpec((qt, LANES), lambda b, p, i: (b * nq + i, p)),
                  pl.BlockSpec((seq, LANES), lambda b, p, i: (b, p)),
                  pl.BlockSpec((seq, LANES), lambda b, p, i: (b, p))],
        out_specs=pl.BlockSpec((qt, LANES), lambda b, p, i: (b * nq + i, p)),
        out_shape=jax.ShapeDtypeStruct((t, SB_WIDTH), BF16),
        scratch_shapes=[pltpu.VMEM((2, qt, LANES), F32), pltpu.VMEM((qt, LANES), F32),
                        pltpu.VMEM((2, 2, qt, 2 * LANES), BF16), pltpu.VMEM((2, 2, qt, LANES), F32)],
        compiler_params=_params("arbitrary", "arbitrary", "arbitrary"),
        name="sb_attn",
    )(q, k, v)


def _gla_kernel(q_ref, k_ref, la_ref, v_ref, g_ref, gn_ref, o_ref):
    ch = GLA_CHUNK
    seq = q_ref.shape[0]
    ri = lax.broadcasted_iota(jnp.int32, (ch, ch), 0)
    ci = lax.broadcasted_iota(jnp.int32, (ch, ch), 1)
    causal = ci <= ri
    tri = jnp.where(causal, 1.0, 0.0).astype(F32)
    lane = lax.broadcasted_iota(jnp.int32, (ch, LANES), 1)
    head0 = lane < 64
    sr = lax.broadcasted_iota(jnp.int32, (2 * LANES, LANES), 0)
    sc = lax.broadcasted_iota(jnp.int32, (2 * LANES, LANES), 1)
    same_head = (sr >= LANES) == (sc >= 64)
    n_pairs = q_ref.shape[1] // LANES

    def pair_chunk(p, r0, state_t):
        kcols = slice(p * LANES, (p + 1) * LANES)
        vcols = slice(2 * p * LANES, 2 * (p + 1) * LANES)
        la = la_ref[pl.ds(r0, ch), kcols]
        bcum = _dot(tri, la, precision=_HIGH)
        b_last = bcum[ch - 1:ch, :]
        q_in = q_ref[pl.ds(r0, ch), kcols] * 0.125 * jnp.exp(bcum)
        kc = k_ref[pl.ds(r0, ch), kcols]
        k_in = (kc * jnp.exp(-bcum)).astype(BF16)
        k_dec = (kc * jnp.exp(b_last - bcum)).astype(BF16)
        vc = v_ref[pl.ds(r0, ch), vcols]
        qb = q_in.astype(BF16)
        zero = jnp.zeros_like(qb)
        outs = []
        for h in range(2):
            qm = jnp.where(head0, qb, zero) if h == 0 else jnp.where(head0, zero, qb)
            sc_h = jnp.where(causal, _dot_nt(qm, k_in), 0.0)
            outs.append(_dot(sc_h.astype(BF16), vc[:, h * LANES:(h + 1) * LANES]))
        o = jnp.concatenate(outs, axis=1) + _dot_nt(qb, state_t.astype(BF16))
        kv_t = lax.dot_general(vc, k_dec, (((0,), (0,)), ((), ())), preferred_element_type=F32)
        state_t = state_t * jnp.exp(b_last) + jnp.where(same_head, kv_t, 0.0)
        halves = []
        for h in range(2):
            oh = o[:, h * LANES:(h + 1) * LANES]
            halves.append(oh * lax.rsqrt(jnp.mean(oh * oh, axis=-1, keepdims=True) + EPS))
        on = jnp.concatenate(halves, axis=1) * gn_ref[:, vcols]
        gg = g_ref[pl.ds(r0, ch), vcols]
        o_ref[pl.ds(r0, ch), vcols] = (on * (gg * jax.nn.sigmoid(gg))).astype(o_ref.dtype)
        return state_t

    def chunk(c, states):
        r0 = pl.multiple_of(c * ch, ch)
        return tuple(pair_chunk(p, r0, states[p]) for p in range(n_pairs))

    lax.fori_loop(0, seq // ch, chunk, tuple(jnp.zeros((2 * LANES, LANES), F32) for _ in range(n_pairs)))


def _gla(qg, kg, la, vg, gg, gla_norm, batch, seq):
    t = qg.shape[0]
    bp = lambda b: (b, 0)
    return pl.pallas_call(
        _gla_kernel,
        grid=(batch,),
        in_specs=[pl.BlockSpec((seq, GLA_KW), bp), pl.BlockSpec((seq, GLA_KW), bp),
                  pl.BlockSpec((seq, GLA_KW), bp), pl.BlockSpec((seq, GLA_VW), bp),
                  pl.BlockSpec((seq, GLA_VW), bp), pl.BlockSpec((1, GLA_VW), lambda b: (0, 0))],
        out_specs=pl.BlockSpec((seq, GLA_VW), bp),
        out_shape=jax.ShapeDtypeStruct((t, GLA_VW), BF16),
        compiler_params=_params("arbitrary"),
        name="gla",
    )(qg, kg, la, vg, gg, gla_norm.reshape(1, -1))


def _ab_out_kernel(x_ref, a_ref, b_ref, wa_ref, wb_ref, o_ref):
    o_ref[...] = x_ref[...] + _dot(a_ref[...], wa_ref[...]) + _dot(b_ref[...], wb_ref[...])


def _ab_out(x2, o_sb, o_g, w_out, tm=512):
    t = x2.shape[0]
    w = w_out.astype(BF16)
    row = lambda i: (i, 0)
    fix = lambda i: (0, 0)
    return pl.pallas_call(
        _ab_out_kernel,
        grid=(t // tm,),
        in_specs=[pl.BlockSpec((tm, D_MODEL), row), pl.BlockSpec((tm, SB_WIDTH), row),
                  pl.BlockSpec((tm, GLA_VW), row), pl.BlockSpec((SB_WIDTH, D_MODEL), fix),
                  pl.BlockSpec((GLA_VW, D_MODEL), fix)],
        out_specs=pl.BlockSpec((tm, D_MODEL), row),
        out_shape=jax.ShapeDtypeStruct((t, D_MODEL), F32),
        compiler_params=_params("arbitrary"),
        name="ab_out",
    )(x2, o_sb, o_g, w[:SB_WIDTH], w[SB_WIDTH:])


def _sgu_kernel(x_ref, g_ref, win_ref, vn_ref, ws_ref, bs_ref, wout_ref, o_ref, y_ref):
    tm = x_ref.shape[0]
    x = x_ref[...]
    h = _rms(x, g_ref[...]).astype(BF16)
    ri = lax.broadcasted_iota(jnp.int32, (SGU_CHUNK, SGU_CHUNK), 0)
    ci = lax.broadcasted_iota(jnp.int32, (SGU_CHUNK, SGU_CHUNK), 1)
    causal = ci <= ri
    v = _gelu(_dot(h, win_ref[:, SGU_WIDTH:]))
    v = (_rms(v, vn_ref[...])).astype(BF16)
    for g in range(SGU_GROUPS):
        cols = slice(g * SGU_GROUP_DIM, (g + 1) * SGU_GROUP_DIM)
        wg = jnp.where(causal, ws_ref[g], 0.0).astype(BF16)
        u_g = _gelu(_dot(h, win_ref[:, cols]))
        bias = bs_ref[g]
        for c in range(tm // SGU_CHUNK):
            rows = slice(c * SGU_CHUNK, (c + 1) * SGU_CHUNK)
            mixed = _dot(wg, v[rows, cols]) + jnp.concatenate([bias, bias], axis=1)
            y_ref[rows, cols] = (u_g[rows] * mixed).astype(BF16)
    o_ref[...] = x + _dot(y_ref[...], wout_ref[...])


def _sgu(x2, g, w_in, v_norm, w_spatial, b_spatial, w_out, tm=256):
    t = x2.shape[0]
    row = lambda i: (i, 0)
    fix = lambda i: (0, 0)
    fix3 = lambda i: (0, 0, 0)
    bias = jnp.broadcast_to(b_spatial[:, :, None], (SGU_GROUPS, SGU_CHUNK, LANES))
    return pl.pallas_call(
        _sgu_kernel,
        grid=(t // tm,),
        in_specs=[pl.BlockSpec((tm, D_MODEL), row), pl.BlockSpec((1, D_MODEL), fix),
                  pl.BlockSpec((D_MODEL, 2 * SGU_WIDTH), fix), pl.BlockSpec((1, SGU_WIDTH), fix),
                  pl.BlockSpec((SGU_GROUPS, SGU_CHUNK, SGU_CHUNK), fix3),
                  pl.BlockSpec((SGU_GROUPS, SGU_CHUNK, LANES), fix3),
                  pl.BlockSpec((SGU_WIDTH, D_MODEL), fix)],
        out_specs=pl.BlockSpec((tm, D_MODEL), row),
        out_shape=jax.ShapeDtypeStruct((t, D_MODEL), F32),
        scratch_shapes=[pltpu.VMEM((tm, SGU_WIDTH), BF16)],
        compiler_params=_params("arbitrary"),
        name="sgu",
    )(x2, g.reshape(1, -1), w_in.astype(BF16), v_norm.reshape(1, -1), w_spatial, bias, w_out.astype(BF16))


def _top16(s, rid):
    ridf = rid.astype(F32)
    vals, idxs = [], []
    for _ in range(PEER_TOPK):
        m = jnp.max(s, axis=0, keepdims=True)
        first = jnp.min(jnp.where(s == m, ridf, jnp.inf), axis=0, keepdims=True)
        vals.append(m)
        idxs.append(first)
        s = jnp.where(ridf == first, -jnp.inf, s)
    return jnp.concatenate(vals, axis=0), jnp.concatenate(idxs, axis=0).astype(jnp.int32)


def _pair_candidates(s0, s1):
    tm = s0.shape[1]
    sub8 = lax.broadcasted_iota(jnp.int32, (SUBLANES, tm), 0)
    sub16 = lax.broadcasted_iota(jnp.int32, (PEER_TOPK, tm), 0)
    vals = [s0[0:1, :] + s1]
    pos = [sub16]
    for a in range(1, 8):
        v = s0[a:a + 1, :] + s1[0:SUBLANES, :]
        vals.append(jnp.where(sub8 < PEER_TOPK // (a + 1), v, -jnp.inf))
        pos.append(sub8 + a * PEER_TOPK)
    vals.append(s0[SUBLANES:, :] + s1[0:1, :])
    pos.append((sub8 + SUBLANES) * PEER_TOPK)
    return jnp.concatenate(vals, axis=0), jnp.concatenate(pos, axis=0)


def _route_kernel(x_ref, g_ref, wq_ref, sk_ref, h_ref, idx_ref, gate_ref):
    tm = x_ref.shape[0]
    h = _rms(x_ref[...], g_ref[...])
    h_ref[...] = h
    q = _dot(h.astype(BF16), wq_ref[...]).astype(BF16)
    k16 = lax.broadcasted_iota(jnp.int32, (PEER_TOPK, tm), 0)
    key_id = lax.broadcasted_iota(jnp.int32, (N_KEYS, tm), 0)
    idx_rows, gate_rows = [], []
    for hd in range(PEER_HEADS):
        top = []
        for p in range(2):
            j = hd * 2 + p
            s = _dot_nt(sk_ref[j], q[:, j * N_KEYS:(j + 1) * N_KEYS])
            top.append(_top16(s, key_id))
        (s0, i0), (s1, i1) = top
        best, pos = _top16(*_pair_candidates(s0, s1))
        rows = []
        for r in range(PEER_TOPK):
            pa = pos[r:r + 1, :] >> 4
            pb = pos[r:r + 1, :] & 15
            ea = jnp.sum(jnp.where(k16 == pa, i0, 0), axis=0, keepdims=True)
            eb = jnp.sum(jnp.where(k16 == pb, i1, 0), axis=0, keepdims=True)
            rows.append(ea * N_KEYS + eb)
        idx_rows.append(jnp.concatenate(rows, axis=0))
        e = jnp.exp(best - best[0:1, :])
        gate_rows.append(e / jnp.sum(e, axis=0, keepdims=True))
    idx_t = jnp.concatenate(idx_rows, axis=0)
    gate_t = jnp.concatenate(gate_rows, axis=0)
    idx_ref[...] = (idx_t * ROWS_PER_EXPERT).T
    gate_ref[...] = gate_t.T


def _route(x2, g, w_q, subkeys, tm=128):
    t = x2.shape[0]
    row = lambda i: (i, 0)
    fix = lambda i: (0, 0)
    sk = subkeys.reshape(PEER_HEADS * 2, N_KEYS, N_KEYS).astype(BF16)
    return pl.pallas_call(
        _route_kernel,
        grid=(t // tm,),
        in_specs=[pl.BlockSpec((tm, D_MODEL), row), pl.BlockSpec((1, D_MODEL), fix),
                  pl.BlockSpec((D_MODEL, 2 * PEER_HEADS * N_KEYS), fix),
                  pl.BlockSpec((PEER_HEADS * 2, N_KEYS, N_KEYS), lambda i: (0, 0, 0))],
        out_specs=[pl.BlockSpec((tm, D_MODEL), row), pl.BlockSpec((tm, PEER_HK), row),
                   pl.BlockSpec((tm, PEER_HK), row)],
        out_shape=[jax.ShapeDtypeStruct((t, D_MODEL), F32), jax.ShapeDtypeStruct((t, PEER_HK), jnp.int32),
                   jax.ShapeDtypeStruct((t, PEER_HK), F32)],
        compiler_params=_params("arbitrary"),
        name="peer_route",
    )(x2, g.reshape(1, -1), w_q.astype(BF16), sk)


def _pack_table(tab):
    n = tab.shape[0]
    b = lax.bitcast_convert_type(tab.astype(BF16), jnp.uint16).astype(jnp.uint32)
    b = b.reshape(n, ROWS_PER_EXPERT, 2, LANES)
    return (b[:, :, 0, :] | (b[:, :, 1, :] << 16)).reshape(n * ROWS_PER_EXPERT, LANES)


def _slot_masks():
    m = jnp.arange(PEER_HK * SUBLANES)
    chunk_of_m = (m[None, :] % SUBLANES) == jnp.arange(SUBLANES)[:, None]
    slot_of_m = (m[None, :] // SUBLANES) == jnp.arange(PEER_HK)[:, None]
    return chunk_of_m, slot_of_m


def _gather_rows(iwin_ref, slot, tab_ref, stage_ref):
    for k in range(PEER_HK):
        src = pl.ds(pl.multiple_of(iwin_ref[slot, k], ROWS_PER_EXPERT), ROWS_PER_EXPERT)
        stage_ref[k * ROWS_PER_EXPERT:(k + 1) * ROWS_PER_EXPERT, :] = tab_ref[src, :]


def _pipelined_tokens(idx_ref, tab_ref, stages, iwin_ref, sems, tb, consume):
    n_stage = len(stages)
    n_grp = IDX_WINDOW // IDX_GROUP

    def idx_copy(first_token, g):
        src = pl.multiple_of(jnp.minimum(first_token, tb - IDX_GROUP), IDX_GROUP)
        return pltpu.make_async_copy(idx_ref.at[pl.ds(src, IDX_GROUP)],
                                     iwin_ref.at[pl.ds(g * IDX_GROUP, IDX_GROUP)], sems.at[g])

    for g in range(n_grp):
        idx_copy(g * IDX_GROUP, g).start()
    idx_copy(0, 0).wait()
    for s in range(n_stage):
        _gather_rows(iwin_ref, s, tab_ref, stages[s])

    def window(n, carry):
        base = IDX_WINDOW * n
        for tau in range(IDX_WINDOW):
            stage = stages[tau % n_stage]
            consume(base + tau, stage)
            row = (tau + n_stage) % IDX_WINDOW
            if row % IDX_GROUP == 0:
                idx_copy(0, row // IDX_GROUP).wait()
            _gather_rows(iwin_ref, row, tab_ref, stage)
            if row % IDX_GROUP == IDX_GROUP - 1:
                g = row // IDX_GROUP
                idx_copy(base + IDX_WINDOW + g * IDX_GROUP, g).start()
        return carry

    lax.fori_loop(0, tb // IDX_WINDOW, window, 0)
    for g in range(1, n_grp):
        idx_copy(0, g).wait()


def _split_bf16(x):
    hi = x.astype(BF16)
    return hi, (x - hi.astype(F32)).astype(BF16)


def _peer_u_kernel(idx_ref, x3_ref, gate_ref, dmask_ref, gsum_ref, tab_ref, w_ref,
                   stage0, stage1, iwin_ref, sems, y_ref):
    tb = x3_ref.shape[0]
    dmask = dmask_ref[...]

    def one(t, stage):
        xh, xl = _split_bf16(x3_ref[t])
        xm = jnp.concatenate([xh, xl], axis=0)
        rows = pltpu.bitcast(stage[...], BF16)
        bt = _dot_nt(xm, rows)
        y_ref[pl.ds(t, 1), :] = jnp.sum(bt * dmask, axis=0, keepdims=True)

    _pipelined_tokens(idx_ref, tab_ref, (stage0, stage1), iwin_ref, sems, tb, one)
    act = _dot(y_ref[...], gsum_ref[...], precision=_HIGH)
    w_ref[...] = gate_ref[...] * _gelu(act)


def _table_spec():
    return pl.BlockSpec((N_EXPERTS * ROWS_PER_EXPERT, LANES), lambda i: (0, 0), pipeline_mode=pl.Buffered(1))


def _peer_u(idx4, h3, gate, tab, tb=PEER_TB):
    t = idx4.shape[0]
    row = lambda i: (i, 0)
    fix = lambda i: (0, 0)
    chunk_of_m, slot_of_m = _slot_masks()
    dmask = jnp.concatenate([chunk_of_m, chunk_of_m], axis=0).astype(F32)
    gsum = slot_of_m.T.astype(F32)
    nm = PEER_HK * SUBLANES
    return pl.pallas_call(
        _peer_u_kernel,
        grid=(t // tb,),
        in_specs=[pl.BlockSpec((tb, PEER_HK), row),
                  pl.BlockSpec((tb, SUBLANES, LANES), lambda i: (i, 0, 0)),
                  pl.BlockSpec((tb, PEER_HK), row),
                  pl.BlockSpec((2 * SUBLANES, nm), fix),
                  pl.BlockSpec((nm, PEER_HK), fix),
                  _table_spec()],
        out_specs=pl.BlockSpec((tb, PEER_HK), row),
        out_shape=jax.ShapeDtypeStruct((t, PEER_HK), F32),
        scratch_shapes=[pltpu.VMEM((PEER_HK * ROWS_PER_EXPERT, LANES), jnp.uint32),
                        pltpu.VMEM((PEER_HK * ROWS_PER_EXPERT, LANES), jnp.uint32),
                        pltpu.SMEM((IDX_WINDOW, PEER_HK), jnp.int32),
                        pltpu.SemaphoreType.DMA((IDX_WINDOW // IDX_GROUP,)),
                        pltpu.VMEM((tb, nm), F32)],
        compiler_params=_params("arbitrary"),
        name="peer_u",
    )(idx4, h3, gate, dmask, gsum, tab)


def _peer_v_kernel(idx_ref, w_ref, x3_ref, expand_ref, dmask_ref, tab_ref, o_ref,
                   stage0, stage1, iwin_ref, sems, w8_ref):
    tb = x3_ref.shape[0]
    wh, wl = _split_bf16(w_ref[...])
    w8_ref[0:tb, :] = _dot(wh, expand_ref[...])
    w8_ref[tb:2 * tb, :] = _dot(wl, expand_ref[...])
    dmask = dmask_ref[...]

    def one(t, stage):
        hi = (w8_ref[pl.ds(t, 1), :] * dmask).astype(BF16)
        lo = (w8_ref[pl.ds(tb + t, 1), :] * dmask).astype(BF16)
        wsel = jnp.concatenate([hi, lo], axis=0)
        rows = pltpu.bitcast(stage[...], BF16)
        o = _dot(wsel, rows)
        o_ref[t] = x3_ref[t] + (o[0:SUBLANES] + o[SUBLANES:])

    _pipelined_tokens(idx_ref, tab_ref, (stage0, stage1), iwin_ref, sems, tb, one)


def _peer_v(idx4, w, x3, tab, tb=PEER_TB):
    t = idx4.shape[0]
    row = lambda i: (i, 0)
    fix = lambda i: (0, 0)
    tile = pl.BlockSpec((tb, SUBLANES, LANES), lambda i: (i, 0, 0))
    chunk_of_m, slot_of_m = _slot_masks()
    nm = PEER_HK * SUBLANES
    return pl.pallas_call(
        _peer_v_kernel,
        grid=(t // tb,),
        in_specs=[pl.BlockSpec((tb, PEER_HK), row),
                  pl.BlockSpec((tb, PEER_HK), row),
                  tile,
                  pl.BlockSpec((PEER_HK, nm), fix),
                  pl.BlockSpec((SUBLANES, nm), fix),
                  _table_spec()],
        out_specs=tile,
        out_shape=jax.ShapeDtypeStruct((t, SUBLANES, LANES), F32),
        scratch_shapes=[pltpu.VMEM((PEER_HK * ROWS_PER_EXPERT, LANES), jnp.uint32),
                        pltpu.VMEM((PEER_HK * ROWS_PER_EXPERT, LANES), jnp.uint32),
                        pltpu.SMEM((IDX_WINDOW, PEER_HK), jnp.int32),
                        pltpu.SemaphoreType.DMA((IDX_WINDOW // IDX_GROUP,)),
                        pltpu.VMEM((2 * tb, nm), F32)],
        compiler_params=_params("arbitrary"),
        name="peer_v",
    )(idx4, w, x3, slot_of_m.astype(BF16), chunk_of_m.astype(F32), tab)


def _peer(x2, g, w_q, subkeys, u_tab, v_tab):
    t = x2.shape[0]
    h, idx4, gate = _route(x2, g, w_q, subkeys)
    w = _peer_u(idx4, h.reshape(t, SUBLANES, LANES), gate, _pack_table(u_tab))
    out3 = _peer_v(idx4, w, x2.reshape(t, SUBLANES, LANES), _pack_table(v_tab))
    return out3.reshape(t, D_MODEL)


def _final_kernel(x_ref, g_ref, o_ref):
    o_ref[...] = _rms(x_ref[...], g_ref[...])


def _final_norm(x2, g, tm=1024):
    t = x2.shape[0]
    return pl.pallas_call(
        _final_kernel,
        grid=(t // tm,),
        in_specs=[pl.BlockSpec((tm, D_MODEL), lambda i: (i, 0)), pl.BlockSpec((1, D_MODEL), lambda i: (0, 0))],
        out_specs=pl.BlockSpec((tm, D_MODEL), lambda i: (i, 0)),
        out_shape=jax.ShapeDtypeStruct((t, D_MODEL), F32),
        compiler_params=_params("arbitrary"),
        name="final_norm",
    )(x2, g.reshape(1, -1))


def kernel(x, norm_mix, norm_ffn, ab_w_in, ab_w_alpha2, ab_b_alpha2, ab_gla_norm, ab_w_out, c_w_in, c_v_norm, c_w_spatial, c_b_spatial, c_w_out, peer_w_q, peer_subkeys, peer_u, peer_v, final_norm):
    batch, seq, d = x.shape
    x2 = x.reshape(batch * seq, d)
    depth = norm_mix.shape[0]
    for i in range(depth):
        j = i // 2
        if i % 2 == 0:
            qs, ks, vs, qg, kg, vg, gg, la = _ab_in(x2, norm_mix[i], ab_w_in[j], ab_w_alpha2[j], ab_b_alpha2[j])
            o_sb = _sb_attn(qs, ks, vs, batch, seq)
            o_g = _gla(qg, kg, la, vg, gg, ab_gla_norm[j], batch, seq)
            x2 = _ab_out(x2, o_sb, o_g, ab_w_out[j])
        else:
            x2 = _sgu(x2, norm_mix[i], c_w_in[j], c_v_norm[j], c_w_spatial[j], c_b_spatial[j], c_w_out[j])
        x2 = _peer(x2, norm_ffn[i], peer_w_q[i], peer_subkeys[i], peer_u[i], peer_v[i])
    return _final_norm(x2, final_norm).reshape(batch, seq, d)
```

```python
import functools

import jax
import jax.numpy as jnp
from jax import lax
from jax.experimental import pallas as pl
from jax.experimental.pallas import tpu as pltpu

F32 = jnp.float32
BF16 = jnp.bfloat16
EPS = 1e-6

D_MODEL = 1024
SB_WIDTH = 512
SB_BLOCK = 128
GLA_KW = 256
GLA_VW = 512
GLA_CHUNK = 64
GLA_GATE_NORM = 16.0
SGU_WIDTH = 2048
SGU_GROUPS = 8
SGU_GROUP_DIM = 256
SGU_CHUNK = 128
PEER_HEADS = 8
PEER_TOPK = 16
N_KEYS = 128
N_EXPERTS = N_KEYS * N_KEYS
PEER_HK = PEER_HEADS * PEER_TOPK
PEER_TB = 128
IDX_WINDOW = 32
IDX_GROUP = 8
LANES = 128
SUBLANES = 8
ROWS_PER_EXPERT = D_MODEL // (2 * LANES)
VMEM_LIMIT = 56 * 1024 * 1024

_HIGH = lax.Precision.HIGHEST


def _params(*sem, vmem=VMEM_LIMIT):
    return pltpu.CompilerParams(dimension_semantics=sem, vmem_limit_bytes=vmem)


def _rms(x, g):
    return x * lax.rsqrt(jnp.mean(x * x, axis=-1, keepdims=True) + EPS) * g


def _softplus(z):
    return jnp.maximum(z, 0.0) + jnp.log(1.0 + jnp.exp(-jnp.abs(z)))


def _gelu(x):
    return 0.5 * x * (1.0 + lax.erf(x * 0.7071067811865476))


def _dot_nt(a, b, **kw):
    return lax.dot_general(a, b, (((1,), (1,)), ((), ())), preferred_element_type=F32, **kw)


def _dot(a, b, **kw):
    return jnp.dot(a, b, preferred_element_type=F32, **kw)


def _ab_in_kernel(x_ref, g_ref, w_ref, wa_ref, w2_ref, b2_ref,
                  qs_ref, ks_ref, vs_ref, qg_ref, kg_ref, vg_ref, gg_ref, la_ref):
    h = _rms(x_ref[...], g_ref[...]).astype(BF16)
    qs_ref[...] = (_dot(h, w_ref[:, 0:512]) * 0.125).astype(BF16)
    ks_ref[...] = _dot(h, w_ref[:, 512:1024]).astype(BF16)
    vs_ref[...] = _dot(h, w_ref[:, 1024:1536]).astype(BF16)
    qg_ref[...] = _dot(h, w_ref[:, 1536:1792])
    kg_ref[...] = _dot(h, w_ref[:, 1792:2048])
    vg_ref[...] = _dot(h, w_ref[:, 2048:2560]).astype(BF16)
    gg_ref[...] = _dot(h, w_ref[:, 2560:3072])
    a_lr = _dot(h, wa_ref[...])
    pre = _dot(a_lr.astype(BF16), w2_ref[...]) + b2_ref[...]
    la_ref[...] = -_softplus(-pre) * (1.0 / GLA_GATE_NORM)


def _ab_in(x2, g, w_in, w_alpha2, b_alpha2, tm=512):
    t = x2.shape[0]
    w_main = w_in[:, :3072].astype(BF16)
    wa = jnp.pad(w_in[:, 3072:], ((0, 0), (0, LANES - 16))).astype(BF16)
    w2 = jnp.pad(w_alpha2, ((0, LANES - 16), (0, 0))).astype(BF16)
    row = lambda i: (i, 0)
    fix = lambda i: (0, 0)
    outs = [(512, BF16), (512, BF16), (512, BF16), (256, F32), (256, F32), (512, BF16), (512, F32), (256, F32)]
    return pl.pallas_call(
        _ab_in_kernel,
        grid=(t // tm,),
        in_specs=[pl.BlockSpec((tm, D_MODEL), row), pl.BlockSpec((1, D_MODEL), fix),
                  pl.BlockSpec((D_MODEL, 3072), fix), pl.BlockSpec((D_MODEL, LANES), fix),
                  pl.BlockSpec((LANES, GLA_KW), fix), pl.BlockSpec((1, GLA_KW), fix)],
        out_specs=[pl.BlockSpec((tm, n), row) for n, _ in outs],
        out_shape=[jax.ShapeDtypeStruct((t, n), d) for n, d in outs],
        compiler_params=_params("arbitrary"),
        name="ab_in",
    )(x2, g.reshape(1, -1), w_main, wa, w2, b_alpha2.reshape(1, -1))


def _sb_kernel(q_ref, k_ref, v_ref, o_ref, c_ref, acc_ref, lsb_ref, lz_ref):
    qi = pl.program_id(2)
    blk = SB_BLOCK
    qt = q_ref.shape[0]
    kb_per_tile = qt // blk
    r2 = lax.broadcasted_iota(jnp.int32, (2 * blk, 2 * blk), 0) & (blk - 1)
    c2 = lax.broadcasted_iota(jnp.int32, (2 * blk, 2 * blk), 1)
    uo = jnp.where((r2 > c2) | (c2 >= blk), 1.0, 0.0).astype(BF16)
    c_ref[...] = jnp.zeros_like(c_ref)
    acc_ref[...] = jnp.zeros_like(acc_ref)

    def causal(j, r0):
        key = lax.broadcasted_iota(jnp.int32, (qt - r0, blk), 1) + j * blk
        qry = lax.broadcasted_iota(jnp.int32, (qt - r0, blk), 0) + (r0 + qi * qt)
        return key < qry

    def logits(j, diag, slot, r0=0):
        kj = k_ref[pl.ds(pl.multiple_of(j * blk, blk), blk), :]
        q = q_ref[r0:, :]
        first_head = lax.broadcasted_iota(jnp.int32, q.shape, 1) < 64
        zero = jnp.zeros_like(q)
        for h in range(2):
            z = _dot_nt(jnp.where(first_head, q, zero) if h == 0 else jnp.where(first_head, zero, q), kj)
            nz = -z
            lg = jnp.log(1.0 + jnp.exp(jnp.minimum(z, nz)))
            ls = jnp.minimum(nz, 0.0) - lg
            if diag:
                ls = jnp.where(causal(j, r0), ls, 0.0)
            ls_hi = ls.astype(BF16)
            ls_lo = (ls - ls_hi.astype(F32)).astype(BF16)
            lsb_ref[slot, h, r0:, :] = jnp.concatenate([ls_hi, ls_lo], axis=1)
            lz_ref[slot, h, r0:, :] = jnp.minimum(z, 0.0) - lg

    def attend(j, diag, slot, r0=0):
        vj = v_ref[pl.ds(pl.multiple_of(j * blk, blk), blk), :]
        pvs = []
        for h in range(2):
            s2 = _dot(lsb_ref[slot, h, r0:, :], uo)
            c = c_ref[h, r0:, :]
            a = jnp.exp(lz_ref[slot, h, r0:, :] + s2[:, :blk] + c)
            if diag:
                a = jnp.where(causal(j, r0), a, 0.0)
            pvs.append(_dot(a.astype(BF16), vj))
            c_ref[h, r0:, :] = c + s2[:, blk:]
        first_head = lax.broadcasted_iota(jnp.int32, pvs[0].shape, 1) < 64
        acc_ref[r0:, :] += jnp.where(first_head, pvs[0], pvs[1])

    last = (qi + 1) * kb_per_tile - 1
    n_full = qi * kb_per_tile
    logits(last, True, 0, (kb_per_tile - 1) * blk)
    for it in range(kb_per_tile):
        r = kb_per_tile - 1 - it
        attend(last - it, True, it % 2, r * blk)
        if r > 0:
            logits(last - it - 1, True, (it + 1) % 2, (r - 1) * blk)
        else:
            logits(jnp.maximum(n_full - 1, 0), False, (it + 1) % 2)

    def full_step(it, carry):
        j = n_full - 1 - it
        slot = (it + kb_per_tile) % 2
        attend(j, False, slot)
        logits(jnp.maximum(j - 1, 0), False, 1 - slot)
        return carry

    lax.fori_loop(0, n_full, full_step, 0)
    o_ref[...] = acc_ref[...].astype(o_ref.dtype)


def _sb_attn(q, k, v, batch, seq, qt=512):
    t = q.shape[0]
    nq = seq // qt
    return pl.pallas_call(
        _sb_kernel,
        grid=(batch, SB_WIDTH // LANES, nq),
        in_specs=[pl.BlockSpec((qt, LANES), lambda b, p, i: (b * nq + i, p)),
                  pl.BlockSpec((seq, LANES), lambda b, p, i: (b, p)),
                  pl.BlockSpec((seq, LANES), lambda b, p, i: (b, p))],
        out_specs=pl.BlockSpec((qt, LANES), lambda b, p, i: (b * nq + i, p)),
        out_shape=jax.ShapeDtypeStruct((t, SB_WIDTH), BF16),
        scratch_shapes=[pltpu.VMEM((2, qt, LANES), F32), pltpu.VMEM((qt, LANES), F32),
                        pltpu.VMEM((2, 2, qt, 2 * LANES), BF16), pltpu.VMEM((2, 2, qt, LANES), F32)],
        compiler_params=_params("arbitrary", "arbitrary", "arbitrary"),
        name="sb_attn",
    )(q, k, v)


def _gla_kernel(q_ref, k_ref, la_ref, v_ref, g_ref, gn_ref, o_ref):
    ch = GLA_CHUNK
    seq = q_ref.shape[0]
    ri = lax.broadcasted_iota(jnp.int32, (ch, ch), 0)
    ci = lax.broadcasted_iota(jnp.int32, (ch, ch), 1)
    causal = ci <= ri
    tri = jnp.where(causal, 1.0, 0.0).astype(F32)
    lane = lax.broadcasted_iota(jnp.int32, (ch, LANES), 1)
    head0 = lane < 64
    sr = lax.broadcasted_iota(jnp.int32, (2 * LANES, LANES), 0)
    sc = lax.broadcasted_iota(jnp.int32, (2 * LANES, LANES), 1)
    same_head = (sr >= LANES) == (sc >= 64)
    n_pairs = q_ref.shape[1] // LANES

    def pair_chunk(p, r0, state_t):
        kcols = slice(p * LANES, (p + 1) * LANES)
        vcols = slice(2 * p * LANES, 2 * (p + 1) * LANES)
        la = la_ref[pl.ds(r0, ch), kcols]
        bcum = _dot(tri, la, precision=_HIGH)
        b_last = bcum[ch - 1:ch, :]
        q_in = q_ref[pl.ds(r0, ch), kcols] * 0.125 * jnp.exp(bcum)
        kc = k_ref[pl.ds(r0, ch), kcols]
        k_in = (kc * jnp.exp(-bcum)).astype(BF16)
        k_dec = (kc * jnp.exp(b_last - bcum)).astype(BF16)
        vc = v_ref[pl.ds(r0, ch), vcols]
        qb = q_in.astype(BF16)
        zero = jnp.zeros_like(qb)
        outs = []
        for h in range(2):
            qm = jnp.where(head0, qb, zero) if h == 0 else jnp.where(head0, zero, qb)
            sc_h = jnp.where(causal, _dot_nt(qm, k_in), 0.0)
            outs.append(_dot(sc_h.astype(BF16), vc[:, h * LANES:(h + 1) * LANES]))
        o = jnp.concatenate(outs, axis=1) + _dot_nt(qb, state_t.astype(BF16))
        kv_t = lax.dot_general(vc, k_dec, (((0,), (0,)), ((), ())), preferred_element_type=F32)
        state_t = state_t * jnp.exp(b_last) + jnp.where(same_head, kv_t, 0.0)
        halves = []
        for h in range(2):
            oh = o[:, h * LANES:(h + 1) * LANES]
            halves.append(oh * lax.rsqrt(jnp.mean(oh * oh, axis=-1, keepdims=True) + EPS))
        on = jnp.concatenate(halves, axis=1) * gn_ref[:, vcols]
        gg = g_ref[pl.ds(r0, ch), vcols]
        o_ref[pl.ds(r0, ch), vcols] = (on * (gg * jax.nn.sigmoid(gg))).astype(o_ref.dtype)
        return state_t

    def chunk(c, states):
        r0 = pl.multiple_of(c * ch, ch)
        return tuple(pair_chunk(p, r0, states[p]) for p in range(n_pairs))

    lax.fori_loop(0, seq // ch, chunk, tuple(jnp.zeros((2 * LANES, LANES), F32) for _ in range(n_pairs)))


def _gla(qg, kg, la, vg, gg, gla_norm, batch, seq):
    t = qg.shape[0]
    bp = lambda b: (b, 0)
    return pl.pallas_call(
        _gla_kernel,
        grid=(batch,),
        in_specs=[pl.BlockSpec((seq, GLA_KW), bp), pl.BlockSpec((seq, GLA_KW), bp),
                  pl.BlockSpec((seq, GLA_KW), bp), pl.BlockSpec((seq, GLA_VW), bp),
                  pl.BlockSpec((seq, GLA_VW), bp), pl.BlockSpec((1, GLA_VW), lambda b: (0, 0))],
        out_specs=pl.BlockSpec((seq, GLA_VW), bp),
        out_shape=jax.ShapeDtypeStruct((t, GLA_VW), BF16),
        compiler_params=_params("arbitrary"),
        name="gla",
    )(qg, kg, la, vg, gg, gla_norm.reshape(1, -1))


def _ab_out_kernel(x_ref, a_ref, b_ref, wa_ref, wb_ref, o_ref):
    o_ref[...] = x_ref[...] + _dot(a_ref[...], wa_ref[...]) + _dot(b_ref[...], wb_ref[...])


def _ab_out(x2, o_sb, o_g, w_out, tm=512):
    t = x2.shape[0]
    w = w_out.astype(BF16)
    row = lambda i: (i, 0)
    fix = lambda i: (0, 0)
    return pl.pallas_call(
        _ab_out_kernel,
        grid=(t // tm,),
        in_specs=[pl.BlockSpec((tm, D_MODEL), row), pl.BlockSpec((tm, SB_WIDTH), row),
                  pl.BlockSpec((tm, GLA_VW), row), pl.BlockSpec((SB_WIDTH, D_MODEL), fix),
                  pl.BlockSpec((GLA_VW, D_MODEL), fix)],
        out_specs=pl.BlockSpec((tm, D_MODEL), row),
        out_shape=jax.ShapeDtypeStruct((t, D_MODEL), F32),
        compiler_params=_params("arbitrary"),
        name="ab_out",
    )(x2, o_sb, o_g, w[:SB_WIDTH], w[SB_WIDTH:])


def _sgu_kernel(x_ref, g_ref, win_ref, vn_ref, ws_ref, bs_ref, wout_ref, o_ref, y_ref):
    tm = x_ref.shape[0]
    x = x_ref[...]
    h = _rms(x, g_ref[...]).astype(BF16)
    ri = lax.broadcasted_iota(jnp.int32, (SGU_CHUNK, SGU_CHUNK), 0)
    ci = lax.broadcasted_iota(jnp.int32, (SGU_CHUNK, SGU_CHUNK), 1)
    causal = ci <= ri
    v = _gelu(_dot(h, win_ref[:, SGU_WIDTH:]))
    v = (_rms(v, vn_ref[...])).astype(BF16)
    for g in range(SGU_GROUPS):
        cols = slice(g * SGU_GROUP_DIM, (g + 1) * SGU_GROUP_DIM)
        wg = jnp.where(causal, ws_ref[g], 0.0).astype(BF16)
        u_g = _gelu(_dot(h, win_ref[:, cols]))
        bias = bs_ref[g]
        for c in range(tm // SGU_CHUNK):
            rows = slice(c * SGU_CHUNK, (c + 1) * SGU_CHUNK)
            mixed = _dot(wg, v[rows, cols]) + jnp.concatenate([bias, bias], axis=1)
            y_ref[rows, cols] = (u_g[rows] * mixed).astype(BF16)
    o_ref[...] = x + _dot(y_ref[...], wout_ref[...])


def _sgu(x2, g, w_in, v_norm, w_spatial, b_spatial, w_out, tm=256):
    t = x2.shape[0]
    row = lambda i: (i, 0)
    fix = lambda i: (0, 0)
    fix3 = lambda i: (0, 0, 0)
    bias = jnp.broadcast_to(b_spatial[:, :, None], (SGU_GROUPS, SGU_CHUNK, LANES))
    return pl.pallas_call(
        _sgu_kernel,
        grid=(t // tm,),
        in_specs=[pl.BlockSpec((tm, D_MODEL), row), pl.BlockSpec((1, D_MODEL), fix),
                  pl.BlockSpec((D_MODEL, 2 * SGU_WIDTH), fix), pl.BlockSpec((1, SGU_WIDTH), fix),
                  pl.BlockSpec((SGU_GROUPS, SGU_CHUNK, SGU_CHUNK), fix3),
                  pl.BlockSpec((SGU_GROUPS, SGU_CHUNK, LANES), fix3),
                  pl.BlockSpec((SGU_WIDTH, D_MODEL), fix)],
        out_specs=pl.BlockSpec((tm, D_MODEL), row),
        out_shape=jax.ShapeDtypeStruct((t, D_MODEL), F32),
        scratch_shapes=[pltpu.VMEM((tm, SGU_WIDTH), BF16)],
        compiler_params=_params("arbitrary"),
        name="sgu",
    )(x2, g.reshape(1, -1), w_in.astype(BF16), v_norm.reshape(1, -1), w_spatial, bias, w_out.astype(BF16))


def _top16(s, rid):
    ridf = rid.astype(F32)
    vals, idxs = [], []
    for _ in range(PEER_TOPK):
        m = jnp.max(s, axis=0, keepdims=True)
        first = jnp.min(jnp.where(s == m, ridf, jnp.inf), axis=0, keepdims=True)
        vals.append(m)
        idxs.append(first)
        s = jnp.where(ridf == first, -jnp.inf, s)
    return jnp.concatenate(vals, axis=0), jnp.concatenate(idxs, axis=0).astype(jnp.int32)


def _pair_candidates(s0, s1):
    tm = s0.shape[1]
    sub8 = lax.broadcasted_iota(jnp.int32, (SUBLANES, tm), 0)
    sub16 = lax.broadcasted_iota(jnp.int32, (PEER_TOPK, tm), 0)
    vals = [s0[0:1, :] + s1]
    pos = [sub16]
    for a in range(1, 8):
        v = s0[a:a + 1, :] + s1[0:SUBLANES, :]
        vals.append(jnp.where(sub8 < PEER_TOPK // (a + 1), v, -jnp.inf))
        pos.append(sub8 + a * PEER_TOPK)
    vals.append(s0[SUBLANES:, :] + s1[0:1, :])
    pos.append((sub8 + SUBLANES) * PEER_TOPK)
    return jnp.concatenate(vals, axis=0), jnp.concatenate(pos, axis=0)


def _route_kernel(x_ref, g_ref, wq_ref, sk_ref, h_ref, idx_ref, gate_ref):
    tm = x_ref.shape[0]
    h = _rms(x_ref[...], g_ref[...])
    h_ref[...] = h
    q = _dot(h.astype(BF16), wq_ref[...]).astype(BF16)
    k16 = lax.broadcasted_iota(jnp.int32, (PEER_TOPK, tm), 0)
    key_id = lax.broadcasted_iota(jnp.int32, (N_KEYS, tm), 0)
    idx_rows, gate_rows = [], []
    for hd in range(PEER_HEADS):
        top = []
        for p in range(2):
            j = hd * 2 + p
            s = _dot_nt(sk_ref[j], q[:, j * N_KEYS:(j + 1) * N_KEYS])
            top.append(_top16(s, key_id))
        (s0, i0), (s1, i1) = top
        best, pos = _top16(*_pair_candidates(s0, s1))
        rows = []
        for r in range(PEER_TOPK):
            pa = pos[r:r + 1, :] >> 4
            pb = pos[r:r + 1, :] & 15
            ea = jnp.sum(jnp.where(k16 == pa, i0, 0), axis=0, keepdims=True)
            eb = jnp.sum(jnp.where(k16 == pb, i1, 0), axis=0, keepdims=True)
            rows.append(ea * N_KEYS + eb)
        idx_rows.append(jnp.concatenate(rows, axis=0))
        e = jnp.exp(best - best[0:1, :])
        gate_rows.append(e / jnp.sum(e, axis=0, keepdims=True))
    idx_t = jnp.concatenate(idx_rows, axis=0)
    gate_t = jnp.concatenate(gate_rows, axis=0)
    idx_ref[...] = (idx_t * ROWS_PER_EXPERT).T
    gate_ref[...] = gate_t.T


def _route(x2, g, w_q, subkeys, tm=128):
    t = x2.shape[0]
    row = lambda i: (i, 0)
    fix = lambda i: (0, 0)
    sk = subkeys.reshape(PEER_HEADS * 2, N_KEYS, N_KEYS).astype(BF16)
    return pl.pallas_call(
        _route_kernel,
        grid=(t // tm,),
        in_specs=[pl.BlockSpec((tm, D_MODEL), row), pl.BlockSpec((1, D_MODEL), fix),
                  pl.BlockSpec((D_MODEL, 2 * PEER_HEADS * N_KEYS), fix),
                  pl.BlockSpec((PEER_HEADS * 2, N_KEYS, N_KEYS), lambda i: (0, 0, 0))],
        out_specs=[pl.BlockSpec((tm, D_MODEL), row), pl.BlockSpec((tm, PEER_HK), row),
                   pl.BlockSpec((tm, PEER_HK), row)],
        out_shape=[jax.ShapeDtypeStruct((t, D_MODEL), F32), jax.ShapeDtypeStruct((t, PEER_HK), jnp.int32),
                   jax.ShapeDtypeStruct((t, PEER_HK), F32)],
        compiler_params=_params("arbitrary"),
        name="peer_route",
    )(x2, g.reshape(1, -1), w_q.astype(BF16), sk)


def _pack_table(tab):
    n = tab.shape[0]
    b = lax.bitcast_convert_type(tab.astype(BF16), jnp.uint16).astype(jnp.uint32)
    b = b.reshape(n, ROWS_PER_EXPERT, 2, LANES)
    return (b[:, :, 0, :] | (b[:, :, 1, :] << 16)).reshape(n * ROWS_PER_EXPERT, LANES)


def _slot_masks():
    m = jnp.arange(PEER_HK * SUBLANES)
    chunk_of_m = (m[None, :] % SUBLANES) == jnp.arange(SUBLANES)[:, None]
    slot_of_m = (m[None, :] // SUBLANES) == jnp.arange(PEER_HK)[:, None]
    return chunk_of_m, slot_of_m


def _gather_rows(iwin_ref, slot, tab_ref):
    tiles = []
    for k in range(0, PEER_HK, 2):
        slabs = [tab_ref[pl.ds(pl.multiple_of(iwin_ref[slot, k + j], ROWS_PER_EXPERT), ROWS_PER_EXPERT), :]
                 for j in range(2)]
        tiles.append(jnp.concatenate(slabs, axis=0))
    return pltpu.bitcast(jnp.concatenate(tiles, axis=0), BF16)


def _windowed_tokens(idx_ref, tab_ref, iwin_ref, sems, tb, consume):
    n_grp = IDX_WINDOW // IDX_GROUP

    def idx_copy(first_token, g):
        src = pl.multiple_of(jnp.minimum(first_token, tb - IDX_GROUP), IDX_GROUP)
        return pltpu.make_async_copy(idx_ref.at[pl.ds(src, IDX_GROUP)],
                                     iwin_ref.at[pl.ds(g * IDX_GROUP, IDX_GROUP)], sems.at[g])

    for g in range(n_grp):
        idx_copy(g * IDX_GROUP, g).start()

    def window(n, carry):
        base = IDX_WINDOW * n
        for tau in range(IDX_WINDOW):
            g = tau // IDX_GROUP
            if tau % IDX_GROUP == 0:
                idx_copy(0, g).wait()
            consume(base + tau, _gather_rows(iwin_ref, tau, tab_ref))
            if tau % IDX_GROUP == IDX_GROUP - 1:
                idx_copy(base + IDX_WINDOW + g * IDX_GROUP, g).start()
        return carry

    lax.fori_loop(0, tb // IDX_WINDOW, window, 0)
    for g in range(n_grp):
        idx_copy(0, g).wait()


def _split_bf16(x):
    hi = x.astype(BF16)
    return hi, (x - hi.astype(F32)).astype(BF16)


def _peer_u_kernel(idx_ref, x3_ref, gate_ref, dmask_ref, gsum_ref, tab_ref, w_ref, iwin_ref, sems, y_ref):
    tb = x3_ref.shape[0]
    dmask = dmask_ref[...]

    def one(t, rows):
        xh, xl = _split_bf16(x3_ref[t])
        xm = jnp.concatenate([xh, xl], axis=0)
        bt = _dot_nt(xm, rows)
        y_ref[pl.ds(t, 1), :] = jnp.sum(bt * dmask, axis=0, keepdims=True)

    _windowed_tokens(idx_ref, tab_ref, iwin_ref, sems, tb, one)
    act = _dot(y_ref[...], gsum_ref[...], precision=_HIGH)
    w_ref[...] = gate_ref[...] * _gelu(act)


def _table_spec():
    return pl.BlockSpec((N_EXPERTS * ROWS_PER_EXPERT, LANES), lambda i: (0, 0), pipeline_mode=pl.Buffered(1))


def _peer_u(idx4, h3, gate, tab, tb=PEER_TB):
    t = idx4.shape[0]
    row = lambda i: (i, 0)
    fix = lambda i: (0, 0)
    chunk_of_m, slot_of_m = _slot_masks()
    dmask = jnp.concatenate([chunk_of_m, chunk_of_m], axis=0).astype(F32)
    gsum = slot_of_m.T.astype(F32)
    nm = PEER_HK * SUBLANES
    return pl.pallas_call(
        _peer_u_kernel,
        grid=(t // tb,),
        in_specs=[pl.BlockSpec((tb, PEER_HK), row),
                  pl.BlockSpec((tb, SUBLANES, LANES), lambda i: (i, 0, 0)),
                  pl.BlockSpec((tb, PEER_HK), row),
                  pl.BlockSpec((2 * SUBLANES, nm), fix),
                  pl.BlockSpec((nm, PEER_HK), fix),
                  _table_spec()],
        out_specs=pl.BlockSpec((tb, PEER_HK), row),
        out_shape=jax.ShapeDtypeStruct((t, PEER_HK), F32),
        scratch_shapes=[pltpu.SMEM((IDX_WINDOW, PEER_HK), jnp.int32),
                        pltpu.SemaphoreType.DMA((IDX_WINDOW // IDX_GROUP,)),
                        pltpu.VMEM((tb, nm), F32)],
        compiler_params=_params("arbitrary"),
        name="peer_u",
    )(idx4, h3, gate, dmask, gsum, tab)


def _peer_v_kernel(idx_ref, w_ref, x3_ref, expand_ref, dmask_ref, tab_ref, o_ref, iwin_ref, sems, w8_ref):
    tb = x3_ref.shape[0]
    wh, wl = _split_bf16(w_ref[...])
    w8_ref[0:tb, :] = _dot(wh, expand_ref[...])
    w8_ref[tb:2 * tb, :] = _dot(wl, expand_ref[...])
    dmask = dmask_ref[...]

    def one(t, rows):
        hi = (w8_ref[pl.ds(t, 1), :] * dmask).astype(BF16)
        lo = (w8_ref[pl.ds(tb + t, 1), :] * dmask).astype(BF16)
        wsel = jnp.concatenate([hi, lo], axis=0)
        o = _dot(wsel, rows)
        o_ref[t] = x3_ref[t] + (o[0:SUBLANES] + o[SUBLANES:])

    _windowed_tokens(idx_ref, tab_ref, iwin_ref, sems, tb, one)


def _peer_v(idx4, w, x3, tab, tb=PEER_TB):
    t = idx4.shape[0]
    row = lambda i: (i, 0)
    fix = lambda i: (0, 0)
    tile = pl.BlockSpec((tb, SUBLANES, LANES), lambda i: (i, 0, 0))
    chunk_of_m, slot_of_m = _slot_masks()
    nm = PEER_HK * SUBLANES
    return pl.pallas_call(
        _peer_v_kernel,
        grid=(t // tb,),
        in_specs=[pl.BlockSpec((tb, PEER_HK), row),
                  pl.BlockSpec((tb, PEER_HK), row),
                  tile,
                  pl.BlockSpec((PEER_HK, nm), fix),
                  pl.BlockSpec((SUBLANES, nm), fix),
                  _table_spec()],
        out_specs=tile,
        out_shape=jax.ShapeDtypeStruct((t, SUBLANES, LANES), F32),
        scratch_shapes=[pltpu.SMEM((IDX_WINDOW, PEER_HK), jnp.int32),
                        pltpu.SemaphoreType.DMA((IDX_WINDOW // IDX_GROUP,)),
                        pltpu.VMEM((2 * tb, nm), F32)],
        compiler_params=_params("arbitrary"),
        name="peer_v",
    )(idx4, w, x3, slot_of_m.astype(BF16), chunk_of_m.astype(F32), tab)


def _peer(x2, g, w_q, subkeys, u_tab, v_tab):
    t = x2.shape[0]
    h, idx4, gate = _route(x2, g, w_q, subkeys)
    w = _peer_u(idx4, h.reshape(t, SUBLANES, LANES), gate, _pack_table(u_tab))
    out3 = _peer_v(idx4, w, x2.reshape(t, SUBLANES, LANES), _pack_table(v_tab))
    return out3.reshape(t, D_MODEL)


def _final_kernel(x_ref, g_ref, o_ref):
    o_ref[...] = _rms(x_ref[...], g_ref[...])


def _final_norm(x2, g, tm=1024):
    t = x2.shape[0]
    return pl.pallas_call(
        _final_kernel,
        grid=(t // tm,),
        in_specs=[pl.BlockSpec((tm, D_MODEL), lambda i: (i, 0)), pl.BlockSpec((1, D_MODEL), lambda i: (0, 0))],
        out_specs=pl.BlockSpec((tm, D_MODEL), lambda i: (i, 0)),
        out_shape=jax.ShapeDtypeStruct((t, D_MODEL), F32),
        compiler_params=_params("arbitrary"),
        name="final_norm",
    )(x2, g.reshape(1, -1))


def kernel(x, norm_mix, norm_ffn, ab_w_in, ab_w_alpha2, ab_b_alpha2, ab_gla_norm, ab_w_out, c_w_in, c_v_norm, c_w_spatial, c_b_spatial, c_w_out, peer_w_q, peer_subkeys, peer_u, peer_v, final_norm):
    batch, seq, d = x.shape
    x2 = x.reshape(batch * seq, d)
    depth = norm_mix.shape[0]
    for i in range(depth):
        j = i // 2
        if i % 2 == 0:
            qs, ks, vs, qg, kg, vg, gg, la = _ab_in(x2, norm_mix[i], ab_w_in[j], ab_w_alpha2[j], ab_b_alpha2[j])
            o_sb = _sb_attn(qs, ks, vs, batch, seq)
            o_g = _gla(qg, kg, la, vg, gg, ab_gla_norm[j], batch, seq)
            x2 = _ab_out(x2, o_sb, o_g, ab_w_out[j])
        else:
            x2 = _sgu(x2, norm_mix[i], c_w_in[j], c_v_norm[j], c_w_spatial[j], c_b_spatial[j], c_w_out[j])
        x2 = _peer(x2, norm_ffn[i], peer_w_q[i], peer_subkeys[i], peer_u[i], peer_v[i])
    return _final_norm(x2, final_norm).reshape(batch, seq, d)
```
